```python
import functools
import jax, jax.numpy as jnp
from jax import lax
import numpy as np

D_MODEL = 2048
BATCH = 8
SEQ = 2048
DEPTH = 2
DEC_BATCH = 128
DEC_SEQ = 4
PAST_LEN = 2048
PAGE_SIZE = 128

D_FF = 5632
FOX_H = 8
FOX_DH = 128
FOX_W = FOX_H * FOX_DH
Q_BLOCK = 128
HG_H = 8
HG_DK = 128
HG_DV = 128
HG_W = HG_H * HG_DK
HG_CHUNK = 64
LRU_W = 1024
LRU_NB = 8
LRU_BS = LRU_W // LRU_NB
CONV_W = 4
LRU_C = 8.0
N_MEM = 256
X_H = 4
X_DH = 128
X_W = X_H * X_DH
EPS = 1e-6
NEG_INF = -1e30
IN_SPLITS = (FOX_W, FOX_W, FOX_W, FOX_H, HG_W, HG_W, HG_H * HG_DV, HG_H * HG_DV, LRU_W, LRU_W, D_MODEL, D_MODEL, D_MODEL)
N_IN = sum(IN_SPLITS)

kernel_name = "hybrid_fox_hgrn2_rglru_decoder_step"

F32 = jnp.float32


def rmsnorm(x, g):
    xf = x.astype(F32)
    y = xf * lax.rsqrt(jnp.mean(xf * xf, axis=-1, keepdims=True) + EPS)
    return (y * g.astype(F32)).astype(x.dtype)


def swiglu_half(x, g, w_gate, w_up, w_down):
    h = rmsnorm(x, g)
    return (jax.nn.silu(h @ w_gate) * (h @ w_up)) @ w_down


def split_cols(z):
    offs = np.cumsum(np.array(IN_SPLITS))[:-1].tolist()
    return jnp.split(z, offs, axis=-1)


def fox_prompt(q, k, v, logf):
    B, T, H, Dh = q.shape
    scale = Dh ** -0.5
    cT = jnp.cumsum(logf, axis=1).transpose(0, 2, 1)
    nb = T // Q_BLOCK
    qb = q.reshape(B, nb, Q_BLOCK, H, Dh).transpose(1, 0, 2, 3, 4)
    cqb = cT.reshape(B, H, nb, Q_BLOCK).transpose(2, 0, 1, 3)
    kpos = jnp.arange(T)

    def one_block(args):
        qi, cqi, bi = args
        s = jnp.einsum('bqhd,bkhd->bhqk', qi, k, preferred_element_type=F32) * scale
        s = s + cqi[..., :, None] - cT[:, :, None, :]
        qpos = bi * Q_BLOCK + jnp.arange(Q_BLOCK)
        s = jnp.where(kpos[None, :] <= qpos[:, None], s, NEG_INF)
        p = jax.nn.softmax(s, axis=-1).astype(v.dtype)
        return jnp.einsum('bhqk,bkhd->bqhd', p, v)

    o = lax.map(one_block, (qb, cqb, jnp.arange(nb)))
    return o.transpose(1, 0, 2, 3, 4).reshape(B, T, H, Dh)


def fox_sample(q, k, v, logf, k_past, v_past, logf_past):
    L = q.shape[1]
    P = k_past.shape[1]
    scale = q.shape[-1] ** -0.5
    cum_new = jnp.cumsum(logf.astype(F32), axis=1).transpose(0, 2, 1)
    lp = logf_past.astype(F32).transpose(0, 2, 1)
    suf = lax.cumsum(lp, axis=2, reverse=True) - lp
    s_past = jnp.einsum('blhd,bphd->bhlp', q, k_past, preferred_element_type=F32) * scale
    s_past = s_past + cum_new[..., :, None] + suf[..., None, :]
    s_new = jnp.einsum('blhd,bmhd->bhlm', q, k, preferred_element_type=F32) * scale
    s_new = s_new + cum_new[..., :, None] - cum_new[..., None, :]
    causal = jnp.tril(jnp.ones((L, L), bool))
    s_new = jnp.where(causal, s_new, NEG_INF)
    p = jax.nn.softmax(jnp.concatenate([s_past, s_new], axis=-1), axis=-1).astype(v.dtype)
    return (jnp.einsum('bhlp,bphd->blhd', p[..., :P], v_past)
            + jnp.einsum('bhlm,bmhd->blhd', p[..., P:], v))


def hgrn2_chunked(q, k, v, logf, S0):
    B, H, T, Dk = q.shape
    Dv = v.shape[-1]
    C = min(HG_CHUNK, T)
    pad = (-T) % C
    if pad:
        pw = ((0, 0), (0, 0), (0, pad), (0, 0))
        q, k, v, logf = [jnp.pad(t, pw) for t in (q, k, v, logf)]
    n = (T + pad) // C

    def chunks(t):
        return t.reshape(B, H, n, C, t.shape[-1]).transpose(2, 0, 1, 3, 4)

    causal = jnp.tril(jnp.ones((C, C), bool))[:, :, None]

    def step(S, xs):
        qc, kc, vc, lc = xs
        b = jnp.cumsum(lc, axis=2)
        o_inter = jnp.einsum('bhtk,bhkv->bhtv', qc * jnp.exp(b), S)
        diff = b[:, :, :, None, :] - b[:, :, None, :, :]
        decay = jnp.where(causal, jnp.exp(jnp.minimum(diff, 0.0)), 0.0)
        A = jnp.einsum('bhtsk,bhsk->bhts', qc[:, :, :, None, :] * decay, kc)
        o = o_inter + jnp.einsum('bhts,bhsv->bhtv', A, vc)
        b_last = b[:, :, -1:, :]
        S_new = (jnp.exp(b_last[:, :, 0, :, None]) * S
                 + jnp.einsum('bhsk,bhsv->bhkv', kc * jnp.exp(b_last - b), vc))
        return S_new, o

    S, o = lax.scan(step, S0.astype(F32), (chunks(q), chunks(k), chunks(v), chunks(logf)))
    o = o.transpose(1, 2, 0, 3, 4).reshape(B, H, n * C, Dv)[:, :, :T]
    return o, S


def causal_dwconv(x, buf, w, b):
    T = x.shape[1]
    xp = jnp.concatenate([buf.astype(x.dtype), x], axis=1)
    y = b + xp[:, 0:T] * w[0]
    for j in range(1, CONV_W):
        y = y + xp[:, j:j + T] * w[j]
    return y, xp[:, -(CONV_W - 1):]


def rglru(xc, h0, w_a, b_a, w_x, b_x, lam):
    B, T, W = xc.shape
    xf = xc.astype(F32)
    xb = xf.reshape(B, T, LRU_NB, LRU_BS)
    r = jax.nn.sigmoid(jnp.einsum('btni,nij->btnj', xb, w_a.astype(F32)).reshape(B, T, W) + b_a.astype(F32))
    ig = jax.nn.sigmoid(jnp.einsum('btni,nij->btnj', xb, w_x.astype(F32)).reshape(B, T, W) + b_x.astype(F32))
    log_a = -LRU_C * r * jax.nn.softplus(-lam.astype(F32))
    a = jnp.exp(log_a)
    u = jnp.sqrt(-jnp.expm1(2.0 * log_a)) * (ig * xf)

    def comb(lhs, rhs):
        a1, b1 = lhs
        a2, b2 = rhs
        return a1 * a2, a2 * b1 + b2

    a_cum, b_cum = lax.associative_scan(comb, (a, u), axis=1)
    h = a_cum * h0.astype(F32)[:, None, :] + b_cum
    return h, h[:, -1]


def token_mixer(x, p, fox_attend, S0, h0, conv0):
    B, T, _ = x.shape
    h = rmsnorm(x, p['mix_norm'])
    (fq, fk, fv, ff, hq, hf, hi, hg, lx, ly, ga, gb, gc) = split_cols(h @ p['w_in'])
    q = rmsnorm(fq.reshape(B, T, FOX_H, FOX_DH), p['fox_q_norm'])
    k = rmsnorm(fk.reshape(B, T, FOX_H, FOX_DH), p['fox_k_norm'])
    v = fv.reshape(B, T, FOX_H, FOX_DH)
    logf_fox = jax.nn.log_sigmoid(ff.astype(F32) + p['fox_b_f'].astype(F32))
    o_fox = fox_attend(q, k, v, logf_fox).reshape(B, T, FOX_W)
    lb = p['hg_lb']
    hff = hf.astype(F32)
    f_g = lb + (1.0 - lb) * jax.nn.sigmoid(hff)
    log_f = jnp.log(f_g)
    hk = (1.0 - lb) * jax.nn.sigmoid(-hff)
    hqf = jax.nn.silu(hq.astype(F32))

    def to_heads(t, d):
        return t.reshape(B, T, HG_H, d).transpose(0, 2, 1, 3)

    o_hg, S_new = hgrn2_chunked(to_heads(hqf, HG_DK), to_heads(hk, HG_DK), to_heads(hi.astype(F32), HG_DV),
                                to_heads(log_f, HG_DK), S0)
    o_hg = rmsnorm(o_hg.transpose(0, 2, 1, 3), p['hg_out_norm'])
    o_hg = (o_hg * jax.nn.silu(hg.astype(F32).reshape(B, T, HG_H, HG_DV))).reshape(B, T, HG_H * HG_DV).astype(x.dtype)
    xc, conv_new = causal_dwconv(lx, conv0, p['lru_conv_w'], p['lru_conv_b'])
    hs, h_last = rglru(xc, h0, p['lru_w_a'], p['lru_b_a'], p['lru_w_x'], p['lru_b_x'], p['lru_lambda'])
    o_lru = (hs * jax.nn.gelu(ly.astype(F32))).astype(x.dtype)
    y = (jax.nn.sigmoid(ga) * (o_fox @ p['w_br_fox'])
         + jax.nn.sigmoid(gb) * (o_hg @ p['w_br_hg'])
         + jax.nn.sigmoid(gc) * (o_lru @ p['w_br_lru']))
    return y @ p['w_out'], (k, v, logf_fox, S_new, h_last, conv_new)


def memory_kv(mem, p):
    B, M, _ = mem.shape
    m = rmsnorm(mem, p['xa_mem_norm'])
    kk, vv = jnp.split(m @ p['xa_w_kv'], 2, axis=-1)
    kk = rmsnorm(kk.reshape(B, M, X_H, X_DH), p['xa_k_norm'])
    return kk, vv.reshape(B, M, X_H, X_DH)


def memory_attend(x, mk, mv, p):
    B, T, _ = x.shape
    h = rmsnorm(x, p['xa_norm'])
    q = rmsnorm((h @ p['xa_w_q']).reshape(B, T, X_H, X_DH), p['xa_q_norm'])
    s = jnp.einsum('bthd,bmhd->bhtm', q, mk, preferred_element_type=F32) * (X_DH ** -0.5)
    pr = jax.nn.softmax(s, axis=-1).astype(mv.dtype)
    o = jnp.einsum('bhtm,bmhd->bthd', pr, mv).reshape(B, T, X_W)
    return o @ p['xa_w_o']


def decoder_layer(x, mk, mv, p, fox_attend, S0, h0, conv0):
    x = x + 0.5 * swiglu_half(x, p['ffn1_norm'], p['ffn1_w_gate'], p['ffn1_w_up'], p['ffn1_w_down'])
    mix, st = token_mixer(x, p, fox_attend, S0, h0, conv0)
    x = x + mix
    x = x + memory_attend(x, mk, mv, p)
    x = x + 0.5 * swiglu_half(x, p['ffn2_norm'], p['ffn2_w_gate'], p['ffn2_w_up'], p['ffn2_w_down'])
    return x, st


def setup_inputs(seed: int = 0) -> dict:
    key = jax.random.key(seed)
    keys = iter(jax.random.split(key, 64))

    def nrm(shape, scale=1.0):
        return jax.random.normal(next(keys), shape, F32) * scale

    def unif(shape, lo, hi):
        return jax.random.uniform(next(keys), shape, F32, lo, hi)

    def gain(shape):
        return 1.0 + nrm(shape, 0.02)

    n_pages = PAST_LEN // PAGE_SIZE
    n_used = DEC_BATCH * n_pages
    n_pool = n_used + max(1, n_used // 4)
    page_table = jax.random.permutation(next(keys), n_pool)[:n_used].reshape(DEC_BATCH, n_pages).astype(jnp.int32)
    D = D_MODEL
    return {
        'x_prompt': nrm((BATCH, SEQ, D)),
        'x_sample': nrm((DEC_BATCH, DEC_SEQ, D)),
        'mem_prompt': nrm((BATCH, N_MEM, D)),
        'cache_k': nrm((DEPTH, n_pool, PAGE_SIZE, FOX_H, FOX_DH)),
        'cache_v': nrm((DEPTH, n_pool, PAGE_SIZE, FOX_H, FOX_DH)),
        'cache_logf': jax.nn.log_sigmoid(unif((DEPTH, n_pool, PAGE_SIZE, FOX_H), 2.0, 5.0) + nrm((DEPTH, n_pool, PAGE_SIZE, FOX_H))),
        'state_hgrn': nrm((DEPTH, DEC_BATCH, HG_H, HG_DK, HG_DV), 0.5),
        'state_lru': nrm((DEPTH, DEC_BATCH, LRU_W)),
        'state_conv': nrm((DEPTH, DEC_BATCH, CONV_W - 1, LRU_W)),
        'cache_mem_k': nrm((DEPTH, DEC_BATCH, N_MEM, X_H, X_DH)),
        'cache_mem_v': nrm((DEPTH, DEC_BATCH, N_MEM, X_H, X_DH)),
        'page_table': page_table,
        'ffn1_norm': gain((DEPTH, D)),
        'ffn1_w_gate': nrm((DEPTH, D, D_FF), D ** -0.5),
        'ffn1_w_up': nrm((DEPTH, D, D_FF), D ** -0.5),
        'ffn1_w_down': nrm((DEPTH, D_FF, D), D_FF ** -0.5),
        'mix_norm': gain((DEPTH, D)),
        'w_in': nrm((DEPTH, D, N_IN), D ** -0.5),
        'fox_b_f': unif((DEPTH, FOX_H), 2.0, 5.0),
        'fox_q_norm': gain((DEPTH, FOX_DH)),
        'fox_k_norm': gain((DEPTH, FOX_DH)),
        'hg_lb_param': nrm((DEPTH, HG_W)),
        'hg_out_norm': gain((DEPTH, HG_DV)),
        'lru_conv_w': nrm((DEPTH, CONV_W, LRU_W), CONV_W ** -0.5),
        'lru_conv_b': nrm((DEPTH, LRU_W), 0.02),
        'lru_w_a': nrm((DEPTH, LRU_NB, LRU_BS, LRU_BS), LRU_BS ** -0.5),
        'lru_b_a': nrm((DEPTH, LRU_W), 0.02),
        'lru_w_x': nrm((DEPTH, LRU_NB, LRU_BS, LRU_BS), LRU_BS ** -0.5),
        'lru_b_x': nrm((DEPTH, LRU_W), 0.02),
        'lru_lambda': unif((DEPTH, LRU_W), 4.3, 9.0),
        'w_br_fox': nrm((DEPTH, FOX_W, D), FOX_W ** -0.5),
        'w_br_hg': nrm((DEPTH, HG_H * HG_DV, D), (HG_H * HG_DV) ** -0.5),
        'w_br_lru': nrm((DEPTH, LRU_W, D), LRU_W ** -0.5),
        'w_out': nrm((DEPTH, D, D), D ** -0.5),
        'xa_norm': gain((DEPTH, D)),
        'xa_mem_norm': gain((DEPTH, D)),
        'xa_w_q': nrm((DEPTH, D, X_W), D ** -0.5),
        'xa_w_kv': nrm((DEPTH, D, 2 * X_W), D ** -0.5),
        'xa_q_norm': gain((DEPTH, X_DH)),
        'xa_k_norm': gain((DEPTH, X_DH)),
        'xa_w_o': nrm((DEPTH, X_W, D), X_W ** -0.5),
        'ffn2_norm': gain((DEPTH, D)),
        'ffn2_w_gate': nrm((DEPTH, D, D_FF), D ** -0.5),
        'ffn2_w_up': nrm((DEPTH, D, D_FF), D ** -0.5),
        'ffn2_w_down': nrm((DEPTH, D_FF, D), D_FF ** -0.5),
    }


def reference(x_prompt, x_sample, mem_prompt, cache_k, cache_v, cache_logf, state_hgrn, state_lru, state_conv,
              cache_mem_k, cache_mem_v, page_table,
              ffn1_norm, ffn1_w_gate, ffn1_w_up, ffn1_w_down,
              mix_norm, w_in, fox_b_f, fox_q_norm, fox_k_norm, hg_lb_param, hg_out_norm,
              lru_conv_w, lru_conv_b, lru_w_a, lru_b_a, lru_w_x, lru_b_x, lru_lambda,
              w_br_fox, w_br_hg, w_br_lru, w_out,
              xa_norm, xa_mem_norm, xa_w_q, xa_w_kv, xa_q_norm, xa_k_norm, xa_w_o,
              ffn2_norm, ffn2_w_gate, ffn2_w_up, ffn2_w_down):
    B = x_prompt.shape[0]
    DB = x_sample.shape[0]
    P = page_table.shape[1] * cache_k.shape[2]
    sm = jax.nn.softmax(hg_lb_param.astype(F32), axis=0)
    hg_lb = jnp.cumsum(sm, axis=0) - sm[0]

    xp, xs = x_prompt, x_sample
    pk_l, pv_l, plf_l, pS_l, ph_l, pc_l, pmk_l, pmv_l = [], [], [], [], [], [], [], []
    sk_l, sv_l, slf_l, sS_l, sh_l, sc_l = [], [], [], [], [], []
    for l in range(DEPTH):
        p = dict(ffn1_norm=ffn1_norm[l], ffn1_w_gate=ffn1_w_gate[l], ffn1_w_up=ffn1_w_up[l], ffn1_w_down=ffn1_w_down[l],
                 mix_norm=mix_norm[l], w_in=w_in[l], fox_b_f=fox_b_f[l], fox_q_norm=fox_q_norm[l], fox_k_norm=fox_k_norm[l],
                 hg_lb=hg_lb[l], hg_out_norm=hg_out_norm[l],
                 lru_conv_w=lru_conv_w[l], lru_conv_b=lru_conv_b[l], lru_w_a=lru_w_a[l], lru_b_a=lru_b_a[l],
                 lru_w_x=lru_w_x[l], lru_b_x=lru_b_x[l], lru_lambda=lru_lambda[l],
                 w_br_fox=w_br_fox[l], w_br_hg=w_br_hg[l], w_br_lru=w_br_lru[l], w_out=w_out[l],
                 xa_norm=xa_norm[l], xa_mem_norm=xa_mem_norm[l], xa_w_q=xa_w_q[l], xa_w_kv=xa_w_kv[l],
                 xa_q_norm=xa_q_norm[l], xa_k_norm=xa_k_norm[l], xa_w_o=xa_w_o[l],
                 ffn2_norm=ffn2_norm[l], ffn2_w_gate=ffn2_w_gate[l], ffn2_w_up=ffn2_w_up[l], ffn2_w_down=ffn2_w_down[l])
        mk, mv = memory_kv(mem_prompt, p)
        xp, (pk, pv, plf, pS, ph, pc) = decoder_layer(
            xp, mk, mv, p, fox_prompt,
            jnp.zeros((B, HG_H, HG_DK, HG_DV), F32), jnp.zeros((B, LRU_W), F32),
            jnp.zeros((B, CONV_W - 1, LRU_W), xp.dtype))
        k_past = cache_k[l][page_table].reshape(DB, P, FOX_H, FOX_DH)
        v_past = cache_v[l][page_table].reshape(DB, P, FOX_H, FOX_DH)
        lf_past = cache_logf[l][page_table].reshape(DB, P, FOX_H)
        fox_s = functools.partial(fox_sample, k_past=k_past, v_past=v_past, logf_past=lf_past)
        xs, (sk, sv, slf, sS, sh, sc) = decoder_layer(
            xs, cache_mem_k[l], cache_mem_v[l], p, fox_s, state_hgrn[l], state_lru[l], state_conv[l])
        pk_l.append(pk); pv_l.append(pv); plf_l.append(plf); pS_l.append(pS); ph_l.append(ph); pc_l.append(pc)
        pmk_l.append(mk); pmv_l.append(mv)
        sk_l.append(sk); sv_l.append(sv); slf_l.append(slf); sS_l.append(sS); sh_l.append(sh); sc_l.append(sc)

    return (xp, xs,
            jnp.stack(pk_l), jnp.stack(pv_l), jnp.stack(plf_l), jnp.stack(pS_l), jnp.stack(ph_l), jnp.stack(pc_l),
            jnp.stack(pmk_l), jnp.stack(pmv_l),
            jnp.stack(sk_l), jnp.stack(sv_l), jnp.stack(slf_l), jnp.stack(sS_l), jnp.stack(sh_l), jnp.stack(sc_l))
```

```python
import functools

import jax
import jax.numpy as jnp
from jax import lax
from jax.experimental import pallas as pl
from jax.experimental.pallas import tpu as pltpu

F32 = jnp.float32
BF16 = jnp.bfloat16
EPS = 1e-6
NEG_INF = -1e30
HD = 128
FOX_H = 8
HG_H = 8
X_H = 4
LRU_C = 8.0
CONV_W = 4
SUB = 8
LANES = 128
SEG = 512
VMEM_LIMIT = 56 * 1024 * 1024
ROW_TILE = 512
PAD_L = 8

NT_DIMS = (((1,), (1,)), ((), ()))
TN_DIMS = (((0,), (0,)), ((), ()))


def _cparams(*sem):
    return pltpu.CompilerParams(dimension_semantics=sem, vmem_limit_bytes=VMEM_LIMIT)


def _rms(x, g):
    return x * lax.rsqrt(jnp.mean(x * x, axis=-1, keepdims=True) + EPS) * g


def _sigmoid(x):
    return jax.nn.sigmoid(x)


def _silu(x):
    return x * jax.nn.sigmoid(x)


def _log_sigmoid(x):
    return jnp.minimum(x, 0.0) - jnp.log1p(jnp.exp(-jnp.abs(x)))


def _softplus(x):
    return jnp.maximum(x, 0.0) + jnp.log1p(jnp.exp(-jnp.abs(x)))


def _gelu_tanh(x):
    return 0.5 * x * (1.0 + jnp.tanh(0.7978845608028654 * (x + 0.044715 * (x * x * x))))


def _cumsum_lanes(x, reverse=False):
    lane = lax.broadcasted_iota(jnp.int32, x.shape, 1)
    d = 1
    while d < LANES:
        if reverse:
            x = x + jnp.where(lane < LANES - d, pltpu.roll(x, LANES - d, 1), 0.0)
        else:
            x = x + jnp.where(lane >= d, pltpu.roll(x, d, 1), 0.0)
        d *= 2
    return x


def _cumsum_rows(x):
    c = x.shape[0]
    row = lax.broadcasted_iota(jnp.int32, (c, 1), 0)
    d = 1
    while d < c:
        x = x + jnp.where(row >= d, pltpu.roll(x, d, 0), 0.0)
        d *= 2
    return x


def _row_to_col(r):
    return jnp.broadcast_to(r, (SUB, r.shape[1])).T[:, 0:1]


def _ffn_body(x_ref, g_ref, wg_ref, wu_ref, wd_ref, o_ref, h_ref, acc_ref):
    j = pl.program_id(1)

    @pl.when(j == 0)
    def _():
        h_ref[...] = _rms(x_ref[...], g_ref[...]).astype(BF16)
        acc_ref[...] = jnp.zeros_like(acc_ref)

    h = h_ref[...]
    gt = jnp.dot(h, wg_ref[...], preferred_element_type=F32)
    ut = jnp.dot(h, wu_ref[...], preferred_element_type=F32)
    a = (_silu(gt) * ut).astype(BF16)
    acc_ref[...] += jnp.dot(a, wd_ref[...], preferred_element_type=F32)

    @pl.when(j == pl.num_programs(1) - 1)
    def _():
        o_ref[...] = x_ref[...] + 0.5 * acc_ref[...]


def _ffn(x, g, wg, wu, wd, tf=512):
    n, d = x.shape
    f = wg.shape[1]
    tm = min(ROW_TILE, n)
    tf = min(tf, f)
    return pl.pallas_call(
        _ffn_body,
        grid=(n // tm, f // tf),
        in_specs=[
            pl.BlockSpec((tm, d), lambda i, j: (i, 0)),
            pl.BlockSpec((1, d), lambda i, j: (0, 0)),
            pl.BlockSpec((d, tf), lambda i, j: (0, j)),
            pl.BlockSpec((d, tf), lambda i, j: (0, j)),
            pl.BlockSpec((tf, d), lambda i, j: (j, 0)),
        ],
        out_specs=pl.BlockSpec((tm, d), lambda i, j: (i, 0)),
        out_shape=jax.ShapeDtypeStruct((n, d), F32),
        scratch_shapes=[pltpu.VMEM((tm, d), BF16), pltpu.VMEM((tm, d), F32)],
        compiler_params=_cparams("parallel", "arbitrary"),
        name="ffn",
    )(x, g.reshape(1, d), wg, wu, wd)


def _nmm_body(x_ref, g_ref, w_ref, gn_ref, o_ref, h_ref, *, n_norm_tiles):
    j = pl.program_id(1)

    @pl.when(j == 0)
    def _():
        h_ref[...] = _rms(x_ref[...], g_ref[...]).astype(BF16)

    y = jnp.dot(h_ref[...], w_ref[...], preferred_element_type=F32)
    if n_norm_tiles == 0:
        o_ref[...] = y
    else:
        @pl.when(j < n_norm_tiles)
        def _():
            tn = y.shape[1]
            parts = [_rms(y[:, c * HD:(c + 1) * HD], gn_ref[:, c * HD:(c + 1) * HD]) for c in range(tn // HD)]
            o_ref[...] = jnp.concatenate(parts, axis=1)

        @pl.when(j >= n_norm_tiles)
        def _():
            o_ref[...] = y


def _norm_matmul(x, g, w, head_gain=None, n_norm_cols=0, tn=SEG):
    n, d = x.shape
    nout = w.shape[1]
    tm = min(ROW_TILE, n)
    tn = min(tn, nout)
    assert n % tm == 0 and nout % tn == 0 and n_norm_cols % tn == 0
    if head_gain is None:
        head_gain = jnp.ones((1, nout), F32)
    return pl.pallas_call(
        functools.partial(_nmm_body, n_norm_tiles=n_norm_cols // tn),
        grid=(n // tm, nout // tn),
        in_specs=[
            pl.BlockSpec((tm, d), lambda i, j: (i, 0)),
            pl.BlockSpec((1, d), lambda i, j: (0, 0)),
            pl.BlockSpec((d, tn), lambda i, j: (0, j)),
            pl.BlockSpec((1, tn), lambda i, j: (0, j)),
        ],
        out_specs=pl.BlockSpec((tm, tn), lambda i, j: (i, j)),
        out_shape=jax.ShapeDtypeStruct((n, nout), F32),
        scratch_shapes=[pltpu.VMEM((tm, d), BF16)],
        compiler_params=_cparams("parallel", "arbitrary"),
        name="norm_matmul",
    )(x, g.reshape(1, d), w, head_gain)


def _mmres_body(a_ref, w_ref, r_ref, o_ref):
    o_ref[...] = r_ref[...] + jnp.dot(a_ref[...], w_ref[...], preferred_element_type=F32)


def _matmul_residual(a, w, res, tn=SEG):
    n, k = a.shape
    nout = w.shape[1]
    tm = min(ROW_TILE, n)
    tn = min(tn, nout)
    return pl.pallas_call(
        _mmres_body,
        grid=(n // tm, nout // tn),
        in_specs=[
            pl.BlockSpec((tm, k), lambda i, j: (i, 0)),
            pl.BlockSpec((k, tn), lambda i, j: (0, j)),
            pl.BlockSpec((tm, tn), lambda i, j: (i, j)),
        ],
        out_specs=pl.BlockSpec((tm, tn), lambda i, j: (i, j)),
        out_shape=jax.ShapeDtypeStruct((n, nout), F32),
        compiler_params=_cparams("parallel", "arbitrary"),
        name="matmul_residual",
    )(a, w, res)


def _merge_body(of_ref, oh_ref, ol_ref, wf_ref, wh_ref, wl_ref, ga_ref, gb_ref, gc_ref, o_ref):
    y = _sigmoid(ga_ref[...]) * jnp.dot(of_ref[...], wf_ref[...], preferred_element_type=F32)
    y = y + _sigmoid(gb_ref[...]) * jnp.dot(oh_ref[...], wh_ref[...], preferred_element_type=F32)
    y = y + _sigmoid(gc_ref[...]) * jnp.dot(ol_ref[...], wl_ref[...], preferred_element_type=F32)
    o_ref[...] = y.astype(o_ref.dtype)


def _merge(o_fox, o_hg, o_lru, wf, wh, wl, z, gate_off, tn=SEG):
    n, k = o_fox.shape
    d = wf.shape[1]
    tm = min(ROW_TILE, n)
    tn = min(tn, d)
    gblk = gate_off // tn
    dblk = d // tn
    act = pl.BlockSpec((tm, k), lambda i, j: (i, 0))
    wsp = pl.BlockSpec((k, tn), lambda i, j: (0, j))
    return pl.pallas_call(
        _merge_body,
        grid=(n // tm, dblk),
        in_specs=[act, act, act, wsp, wsp, wsp,
                  pl.BlockSpec((tm, tn), lambda i, j: (i, gblk + j)),
                  pl.BlockSpec((tm, tn), lambda i, j: (i, gblk + dblk + j)),
                  pl.BlockSpec((tm, tn), lambda i, j: (i, gblk + 2 * dblk + j))],
        out_specs=pl.BlockSpec((tm, tn), lambda i, j: (i, j)),
        out_shape=jax.ShapeDtypeStruct((n, d), BF16),
        compiler_params=_cparams("parallel", "arbitrary"),
        name="merge",
    )(o_fox, o_hg, o_lru, wf, wh, wl, z, z, z)


def _fox_prep_body(zf_ref, bias_ref, lf_ref, c_ref):
    t = zf_ref.shape[0]
    lf = _log_sigmoid(zf_ref[:, 0:LANES] + bias_ref[...])
    lf_ref[0] = lf[:, 0:FOX_H]
    x = lf.T[0:FOX_H, :]
    carry = jnp.zeros((FOX_H, 1), F32)
    for blk in range(t // LANES):
        cs = _cumsum_lanes(x[:, blk * LANES:(blk + 1) * LANES]) + carry
        c_ref[0, :, blk * LANES:(blk + 1) * LANES] = cs
        carry = cs[:, LANES - 1:LANES]


def _fox_prep(z, bias, nb, t, ff_off):
    return pl.pallas_call(
        _fox_prep_body,
        grid=(nb,),
        in_specs=[pl.BlockSpec((t, SEG), lambda b: (b, ff_off // SEG)),
                  pl.BlockSpec((1, LANES), lambda b: (0, 0))],
        out_specs=[pl.BlockSpec((1, t, FOX_H), lambda b: (b, 0, 0)),
                   pl.BlockSpec((1, FOX_H, t), lambda b: (b, 0, 0))],
        out_shape=[jax.ShapeDtypeStruct((nb, t, FOX_H), F32),
                   jax.ShapeDtypeStruct((nb, FOX_H, t), F32)],
        compiler_params=_cparams("parallel"),
        name="fox_prep",
    )(z, bias)


def _fox_prompt_body(q_ref, k_ref, v_ref, c_ref, o_ref, *, tq):
    t = q_ref.shape[0]
    scale = HD ** -0.5
    c = c_ref[0, 0]
    ccol = _row_to_col(c)
    k = k_ref[...].astype(BF16)
    v = v_ref[...].astype(BF16)
    for i in range(t // tq):
        n_keys = (i + 1) * tq
        q = (q_ref[i * tq:(i + 1) * tq, :] * scale).astype(BF16)
        s = lax.dot_general(q, k[0:n_keys], NT_DIMS, preferred_element_type=F32)
        s = s + ccol[i * tq:(i + 1) * tq] - c[:, 0:n_keys]
        row = lax.broadcasted_iota(jnp.int32, s.shape, 0) + i * tq
        col = lax.broadcasted_iota(jnp.int32, s.shape, 1)
        s = jnp.where(col <= row, s, NEG_INF)
        m = jnp.max(s, axis=-1, keepdims=True)
        p = jnp.exp(s - m)
        l = jnp.sum(p, axis=-1, keepdims=True)
        o = jnp.dot(p.astype(BF16), v[0:n_keys], preferred_element_type=F32) / l
        o_ref[i * tq:(i + 1) * tq, :] = o.astype(o_ref.dtype)


def _fox_prompt(z, c_row, nb, t, offs):
    qb, kb, vb = (offs[n] // HD for n in ("fq", "fk", "fv"))
    tq = min(256, t)
    return pl.pallas_call(
        functools.partial(_fox_prompt_body, tq=tq),
        grid=(nb, FOX_H),
        in_specs=[pl.BlockSpec((t, HD), lambda b, h: (b, qb + h)),
                  pl.BlockSpec((t, HD), lambda b, h: (b, kb + h)),
                  pl.BlockSpec((t, HD), lambda b, h: (b, vb + h)),
                  pl.BlockSpec((1, 1, 1, t), lambda b, h: (b, h, 0, 0))],
        out_specs=pl.BlockSpec((t, HD), lambda b, h: (b, h)),
        out_shape=jax.ShapeDtypeStruct((nb * t, FOX_H * HD), BF16),
        compiler_params=_cparams("parallel", "parallel"),
        name="fox_prompt",
    )(z, z, z, c_row.reshape(nb, FOX_H, 1, t))


def _head_mask(nh):
    r = lax.broadcasted_iota(jnp.int32, (SUB, nh * HD), 0)
    c = lax.broadcasted_iota(jnp.int32, (SUB, nh * HD), 1)
    return (c >> 7) == r


def _blockdiag_rows(q, n_tok, nh):
    mask = _head_mask(nh)
    parts = [jnp.where(mask, jnp.broadcast_to(q[l:l + 1, :], (SUB, nh * HD)), 0.0) for l in range(n_tok)]
    return jnp.concatenate(parts, axis=0)


def _blockdiag_fold(acc, n_tok, nh):
    mask = _head_mask(nh)
    return [jnp.sum(jnp.where(mask, acc[l * SUB:(l + 1) * SUB, :], 0.0), axis=0, keepdims=True)
            for l in range(n_tok)]


def _fox_sample_body(pt_ref, zq_ref, zk_ref, zv_ref, zf_ref, bias_ref, kp_ref, vp_ref, lfp_ref,
                     o_ref, lfo_ref, qbd_ref, cum_ref, m_ref, l_ref, acc_ref, carry_ref, *, n_tok):
    p = pl.program_id(1)
    nrow = n_tok * SUB
    w = FOX_H * HD

    @pl.when(p == 0)
    def _():
        scale = HD ** -0.5
        qbd = _blockdiag_rows(zq_ref[0] * scale, n_tok, FOX_H)
        qbd_ref[...] = qbd.astype(BF16)
        tok = lax.broadcasted_iota(jnp.int32, (SUB, LANES), 0)
        lf = _log_sigmoid(zf_ref[0][:, 0:LANES] + bias_ref[...])
        lfo_ref[0] = lf
        cum = _cumsum_rows(jnp.where(tok < n_tok, lf, 0.0))
        cum_t = cum.T[0:FOX_H, 0:SUB]
        cum_col = jnp.concatenate([cum_t[:, l:l + 1] for l in range(n_tok)], axis=0)
        cum_ref[...] = cum_col
        cum_keys = jnp.concatenate([cum_t] * n_tok, axis=0)
        s = lax.dot_general(qbd_ref[...], zk_ref[0].astype(BF16), NT_DIMS, preferred_element_type=F32)
        s = s + cum_col - cum_keys
        qtok = lax.broadcasted_iota(jnp.int32, (nrow, SUB), 0) >> 3
        ktok = lax.broadcasted_iota(jnp.int32, (nrow, SUB), 1)
        s = jnp.where(ktok <= qtok, s, NEG_INF)
        m = jnp.max(s, axis=-1, keepdims=True)
        pn = jnp.exp(s - m)
        m_ref[...] = m
        l_ref[...] = jnp.sum(pn, axis=-1, keepdims=True)
        vn = zv_ref[0]
        acc = jnp.zeros((nrow, w), F32)
        for t in range(n_tok):
            acc = acc + pn[:, t:t + 1] * vn[t:t + 1, :]
        acc_ref[...] = acc
        carry_ref[...] = jnp.zeros_like(carry_ref)

    kp = kp_ref[0, 0].astype(BF16)
    vp = vp_ref[0, 0].astype(BF16)
    s = lax.dot_general(qbd_ref[...], kp, NT_DIMS, preferred_element_type=F32)
    lft = lfp_ref[0, 0].T
    incl = _cumsum_lanes(lft, reverse=True)
    suf = incl - lft + carry_ref[...]
    carry_ref[...] = carry_ref[...] + incl[:, 0:1]
    s = s + cum_ref[...] + jnp.concatenate([suf] * n_tok, axis=0)
    m_old = m_ref[...]
    m_new = jnp.maximum(m_old, jnp.max(s, axis=-1, keepdims=True))
    alpha = jnp.exp(m_old - m_new)
    pp = jnp.exp(s - m_new)
    m_ref[...] = m_new
    l_ref[...] = alpha * l_ref[...] + jnp.sum(pp, axis=-1, keepdims=True)
    acc_ref[...] = alpha * acc_ref[...] + jnp.dot(pp.astype(BF16), vp, preferred_element_type=F32)

    @pl.when(p == pl.num_programs(1) - 1)
    def _():
        o_ref[...] = jnp.zeros_like(o_ref)
        rows = _blockdiag_fold(acc_ref[...] / l_ref[...], n_tok, FOX_H)
        for t in range(n_tok):
            o_ref[0, t:t + 1, :] = rows[t].astype(o_ref.dtype)


def _fox_sample(zs, bias, cache_k, cache_v, cache_logf, page_table, layer, offs, n_tok):
    db = zs.shape[0]
    n_pages = page_table.shape[1]
    page = cache_k.shape[2]
    w = FOX_H * HD
    nrow = n_tok * SUB
    ck = cache_k.reshape(cache_k.shape[0], cache_k.shape[1], page, w)
    cv = cache_v.reshape(cache_v.shape[0], cache_v.shape[1], page, w)
    qb, kb, vb, fb = offs["fq"] // w, offs["fk"] // w, offs["fv"] // w, offs["ff"] // SEG

    def page_map(b, p, pt):
        return (layer, pt[b * n_pages + (n_pages - 1 - p)], 0, 0)

    grid_spec = pltpu.PrefetchScalarGridSpec(
        num_scalar_prefetch=1,
        grid=(db, n_pages),
        in_specs=[pl.BlockSpec((1, PAD_L, w), lambda b, p, pt: (b, 0, qb)),
                  pl.BlockSpec((1, PAD_L, w), lambda b, p, pt: (b, 0, kb)),
                  pl.BlockSpec((1, PAD_L, w), lambda b, p, pt: (b, 0, vb)),
                  pl.BlockSpec((1, PAD_L, SEG), lambda b, p, pt: (b, 0, fb)),
                  pl.BlockSpec((1, LANES), lambda b, p, pt: (0, 0)),
                  pl.BlockSpec((1, 1, page, w), page_map),
                  pl.BlockSpec((1, 1, page, w), page_map),
                  pl.BlockSpec((1, 1, page, FOX_H), page_map)],
        out_specs=[pl.BlockSpec((1, PAD_L, w), lambda b, p, pt: (b, 0, 0)),
                   pl.BlockSpec((1, PAD_L, LANES), lambda b, p, pt: (b, 0, 0))],
        scratch_shapes=[pltpu.VMEM((nrow, w), BF16), pltpu.VMEM((nrow, 1), F32), pltpu.VMEM((nrow, 1), F32),
                        pltpu.VMEM((nrow, 1), F32), pltpu.VMEM((nrow, w), F32), pltpu.VMEM((FOX_H, 1), F32)],
    )
    return pl.pallas_call(
        functools.partial(_fox_sample_body, n_tok=n_tok),
        grid_spec=grid_spec,
        out_shape=[jax.ShapeDtypeStruct((db, PAD_L, w), BF16),
                   jax.ShapeDtypeStruct((db, PAD_L, LANES), F32)],
        compiler_params=_cparams("parallel", "arbitrary"),
        name="fox_sample",
    )(page_table.reshape(-1), zs, zs, zs, zs, bias, ck, cv, cache_logf)


def _hg_chunk(q, k, v, lf, s0):
    c = q.shape[0]
    b = _cumsum_rows(lf)
    o = jnp.dot((q * jnp.exp(b)).astype(BF16), s0.astype(BF16), preferred_element_type=F32)
    vb = v.astype(BF16)

    row = lax.broadcasted_iota(jnp.int32, (c, 1), 0)
    a_mat = None
    half = SUB
    while 2 * half <= c:
        nblk = c // (2 * half)
        shift = (2 * half).bit_length() - 1
        split = jnp.broadcast_to(b.reshape(nblk, 2 * half, HD)[:, half - 1:half, :],
                                 (nblk, 2 * half, HD)).reshape(c, HD)
        right = (row & (2 * half - 1)) >= half
        qt = jnp.where(right, q * jnp.exp(jnp.minimum(b - split, 0.0)), 0.0).astype(BF16)
        kt = jnp.where(right, 0.0, k * jnp.exp(jnp.minimum(split - b, 0.0))).astype(BF16)
        a_l = lax.dot_general(qt, kt, NT_DIMS, preferred_element_type=F32)
        ri = lax.broadcasted_iota(jnp.int32, (c, c), 0) >> shift
        ci = lax.broadcasted_iota(jnp.int32, (c, c), 1) >> shift
        a_l = jnp.where(ri == ci, a_l, 0.0)
        a_mat = a_l if a_mat is None else a_mat + a_l
        half *= 2
    if a_mat is not None:
        o = o + jnp.dot(a_mat.astype(BF16), vb, preferred_element_type=F32)

    sub_row = lax.broadcasted_iota(jnp.int32, (SUB, 1), 0)
    diag = []
    for blk in range(c // SUB):
        sl = slice(blk * SUB, (blk + 1) * SUB)
        qb_, kb_, bb_, vb_ = q[sl], k[sl], b[sl], v[sl]
        ob = jnp.zeros((SUB, HD), F32)
        for s in range(SUB):
            wgt = jnp.exp(jnp.minimum(bb_ - bb_[s:s + 1, :], 0.0))
            a_s = jnp.sum(qb_ * kb_[s:s + 1, :] * wgt, axis=-1, keepdims=True)
            ob = ob + jnp.where(sub_row >= s, a_s, 0.0) * vb_[s:s + 1, :]
        diag.append(ob)
    o = o + (diag[0] if len(diag) == 1 else jnp.concatenate(diag, axis=0))

    b_last = b[c - 1:c, :]
    kt = (k * jnp.exp(b_last - b)).astype(BF16)
    s_new = _row_to_col(jnp.exp(b_last)) * s0 + lax.dot_general(kt, vb, TN_DIMS, preferred_element_type=F32)
    return o, s_new


def _hg_gates(hq, hf, lb):
    sig = _sigmoid(hf)
    lf = jnp.log(lb + (1.0 - lb) * sig)
    k = (1.0 - lb) * _sigmoid(-hf)
    return _silu(hq), k, lf


def _hg_prompt_body(hq_ref, hf_ref, hi_ref, hg_ref, lb_ref, gn_ref, o_ref, s_ref, *, chunk):
    t = hq_ref.shape[0]
    lb = lb_ref[...]
    gn = gn_ref[...]

    def step(ci, s):
        rows = pl.ds(pl.multiple_of(ci * chunk, chunk), chunk)
        q, k, lf = _hg_gates(hq_ref[rows, :], hf_ref[rows, :], lb)
        o, s = _hg_chunk(q, k, hi_ref[rows, :], lf, s)
        o_ref[rows, :] = (_rms(o, gn) * _silu(hg_ref[rows, :])).astype(o_ref.dtype)
        return s

    s_ref[0, 0] = lax.fori_loop(0, t // chunk, step, jnp.zeros((HD, HD), F32))


def _hg_prompt(z, lb, gn, nb, t, offs):
    qb, fb, ib, gb = (offs[n] // HD for n in ("hq", "hf", "hi", "hg"))
    chunk = min(128, t)
    col = lambda base: pl.BlockSpec((t, HD), lambda b, h: (b, base + h))
    return pl.pallas_call(
        functools.partial(_hg_prompt_body, chunk=chunk),
        grid=(nb, HG_H),
        in_specs=[col(qb), col(fb), col(ib), col(gb),
                  pl.BlockSpec((1, HD), lambda b, h: (0, h)),
                  pl.BlockSpec((1, HD), lambda b, h: (0, 0))],
        out_specs=[pl.BlockSpec((t, HD), lambda b, h: (b, h)),
                   pl.BlockSpec((1, 1, HD, HD), lambda b, h: (b, h, 0, 0))],
        out_shape=[jax.ShapeDtypeStruct((nb * t, HG_H * HD), BF16),
                   jax.ShapeDtypeStruct((nb, HG_H, HD, HD), F32)],
        compiler_params=_cparams("parallel", "parallel"),
        name="hgrn2_prompt",
    )(z, z, z, z, lb, gn)


def _hg_sample_body(hq_ref, hf_ref, hi_ref, hg_ref, lb_ref, gn_ref, s0_ref, o_ref, s_ref, *, n_tok):
    valid = lax.broadcasted_iota(jnp.int32, (PAD_L, 1), 0) < n_tok
    gn = gn_ref[...]
    for h in range(HG_H):
        cs = slice(h * HD, (h + 1) * HD)
        q, k, lf = _hg_gates(hq_ref[0, :, cs], hf_ref[0, :, cs], lb_ref[:, cs])
        k = jnp.where(valid, k, 0.0)
        lf = jnp.where(valid, lf, 0.0)
        o, s = _hg_chunk(q, k, hi_ref[0, :, cs], lf, s0_ref[0, 0, h])
        s_ref[0, h] = s
        o_ref[0, :, cs] = (_rms(o, gn) * _silu(hg_ref[0, :, cs])).astype(o_ref.dtype)


def _hg_sample(zs, lb, gn, state, layer, offs, n_tok):
    db = zs.shape[0]
    w = HG_H * HD
    col = lambda name: pl.BlockSpec((1, PAD_L, w), lambda b: (b, 0, offs[name] // w))
    return pl.pallas_call(
        functools.partial(_hg_sample_body, n_tok=n_tok),
        grid=(db,),
        in_specs=[col("hq"), col("hf"), col("hi"), col("hg"),
                  pl.BlockSpec((1, w), lambda b: (0, 0)),
                  pl.BlockSpec((1, HD), lambda b: (0, 0)),
                  pl.BlockSpec((1, 1, HG_H, HD, HD), lambda b: (layer, b, 0, 0, 0))],
        out_specs=[pl.BlockSpec((1, PAD_L, w), lambda b: (b, 0, 0)),
                   pl.BlockSpec((1, HG_H, HD, HD), lambda b: (b, 0, 0, 0))],
        out_shape=[jax.ShapeDtypeStruct((db, PAD_L, w), BF16),
                   jax.ShapeDtypeStruct((db, HG_H, HD, HD), F32)],
        compiler_params=_cparams("parallel"),
        name="hgrn2_sample",
    )(zs, zs, zs, zs, lb, gn, state)


def _lru_gates(xc, wa_ref, ba, wx_ref, bx, lam):
    nblk = xc.shape[1] // HD
    xb = xc.astype(BF16)
    pa = [jnp.dot(xb[:, i * HD:(i + 1) * HD], wa_ref[i], preferred_element_type=F32) for i in range(nblk)]
    px = [jnp.dot(xb[:, i * HD:(i + 1) * HD], wx_ref[i], preferred_element_type=F32) for i in range(nblk)]
    r = _sigmoid(jnp.concatenate(pa, axis=1) + ba)
    ig = _sigmoid(jnp.concatenate(px, axis=1) + bx)
    log_a = -LRU_C * r * _softplus(-lam)
    a = jnp.exp(log_a)
    u = jnp.sqrt(-jnp.tanh(log_a) * (1.0 + a * a)) * (ig * xc)
    return a, u


def _lru_prompt_body(lx_ref, ly_ref, cw_ref, cb_ref, wa_ref, ba_ref, wx_ref, bx_ref, lam_ref,
                     o_ref, h_ref, *, rt):
    t, wt = lx_ref.shape
    cw = cw_ref[...]
    sub_row = lax.broadcasted_iota(jnp.int32, (SUB, 1), 0)
    pos = lax.broadcasted_iota(jnp.int32, (rt, 1), 0) & (SUB - 1)

    def tile(ti, carry):
        h, tail = carry
        rows = pl.ds(pl.multiple_of(ti * rt, rt), rt)
        x = lx_ref[rows, :]
        xc = cb_ref[...] + x * cw[CONV_W - 1:CONV_W, :]
        for d in range(1, CONV_W):
            xr = pltpu.roll(x, d, 0)
            head = jnp.where(sub_row < d, pltpu.roll(tail, d, 0), xr[0:SUB])
            xs = jnp.concatenate([head, xr[SUB:]], axis=0) if rt > SUB else head
            xc = xc + xs * cw[CONV_W - 1 - d:CONV_W - d, :]
        a, u = _lru_gates(xc, wa_ref, ba_ref[...], wx_ref, bx_ref[...], lam_ref[...])
        d = 1
        while d < SUB:
            keep = pos >= d
            u = a * jnp.where(keep, pltpu.roll(u, d, 0), 0.0) + u
            a = a * jnp.where(keep, pltpu.roll(a, d, 0), 1.0)
            d *= 2
        gelu = _gelu_tanh(ly_ref[rows, :])
        for g in range(rt // SUB):
            sl = slice(g * SUB, (g + 1) * SUB)
            hs = a[sl] * h + u[sl]
            h = hs[SUB - 1:SUB, :]
            o_ref[pl.ds(pl.multiple_of(ti * rt, rt) + g * SUB, SUB), :] = (hs * gelu[sl]).astype(o_ref.dtype)
        return h, x[rt - SUB:rt, :]

    h, _ = lax.fori_loop(0, t // rt, tile, (jnp.zeros((1, wt), F32), jnp.zeros((SUB, wt), F32)))
    h_ref[0] = h


def _lru_prompt(z, p, nb, t, offs):
    w = p["lru_conv_b"].shape[1]
    wt = SEG
    nblk = wt // HD
    rt = min(128, t)
    xb, yb = offs["lx"] // wt, offs["ly"] // wt
    vec = pl.BlockSpec((1, wt), lambda b, c: (0, c))
    gate_w = pl.BlockSpec((nblk, HD, HD), lambda b, c: (c, 0, 0))
    return pl.pallas_call(
        functools.partial(_lru_prompt_body, rt=rt),
        grid=(nb, w // wt),
        in_specs=[pl.BlockSpec((t, wt), lambda b, c: (b, xb + c)),
                  pl.BlockSpec((t, wt), lambda b, c: (b, yb + c)),
                  pl.BlockSpec((CONV_W, wt), lambda b, c: (0, c)),
                  vec, gate_w, vec, gate_w, vec, vec],
        out_specs=[pl.BlockSpec((t, wt), lambda b, c: (b, c)),
                   pl.BlockSpec((1, 1, wt), lambda b, c: (b, 0, c))],
        out_shape=[jax.ShapeDtypeStruct((nb * t, w), BF16),
                   jax.ShapeDtypeStruct((nb, 1, w), F32)],
        compiler_params=_cparams("parallel", "parallel"),
        name="lru_prompt",
    )(z, z, p["lru_conv_w"], p["lru_conv_b"], p["lru_w_a"], p["lru_b_a"], p["lru_w_x"], p["lru_b_x"],
      p["lru_lambda"])


def _lru_sample_body(*refs, n_tok):
    lx = refs[0:n_tok]
    ly = refs[n_tok:2 * n_tok]
    buf = refs[2 * n_tok:2 * n_tok + CONV_W - 1]
    h0_ref, cw_ref, cb_ref, wa_ref, ba_ref, wx_ref, bx_ref, lam_ref = refs[2 * n_tok + CONV_W - 1:2 * n_tok + CONV_W + 7]
    outs = refs[2 * n_tok + CONV_W + 7:]
    o_refs, h_ref = outs[0:n_tok], outs[n_tok]
    cw = cw_ref[...]
    xp = [r[0] for r in buf] + [r[...] for r in lx]
    h = h0_ref[0]
    for t in range(n_tok):
        xc = cb_ref[...]
        for j in range(CONV_W):
            xc = xc + xp[t + j] * cw[j:j + 1, :]
        a, u = _lru_gates(xc, wa_ref, ba_ref[...], wx_ref, bx_ref[...], lam_ref[...])
        h = a * h + u
        o_refs[t][...] = (h * _gelu_tanh(ly[t][...])).astype(o_refs[t].dtype)
    h_ref[...] = h


def _lru_sample(zs2, p, state_lru, state_conv, layer, offs, n_tok, nz):
    db = zs2.shape[0]
    w = p["lru_conv_b"].shape[1]
    wt = SEG
    nblk = wt // HD
    sc = state_conv.reshape(state_conv.shape[0], db, (CONV_W - 1) * w)
    tok = lambda name, l: pl.BlockSpec((db, wt), lambda c: (0, (l * nz + offs[name]) // wt + c))
    vec = pl.BlockSpec((1, wt), lambda c: (0, c))
    gate_w = pl.BlockSpec((nblk, HD, HD), lambda c: (c, 0, 0))
    in_specs = ([tok("lx", l) for l in range(n_tok)] + [tok("ly", l) for l in range(n_tok)]
                + [pl.BlockSpec((1, db, wt), lambda c, j=j: (layer, 0, j * (w // wt) + c)) for j in range(CONV_W - 1)]
                + [pl.BlockSpec((1, db, wt), lambda c: (layer, 0, c)),
                   pl.BlockSpec((CONV_W, wt), lambda c: (0, c)),
                   vec, gate_w, vec, gate_w, vec, vec])
    out_spec = pl.BlockSpec((db, wt), lambda c: (0, c))
    outs = pl.pallas_call(
        functools.partial(_lru_sample_body, n_tok=n_tok),
        grid=(w // wt,),
        in_specs=in_specs,
        out_specs=[out_spec] * (n_tok + 1),
        out_shape=[jax.ShapeDtypeStruct((db, w), BF16)] * n_tok + [jax.ShapeDtypeStruct((db, w), F32)],
        compiler_params=_cparams("parallel"),
        name="lru_sample",
    )(*([zs2] * (2 * n_tok) + [sc] * (CONV_W - 1)
        + [state_lru, p["lru_conv_w"], p["lru_conv_b"], p["lru_w_a"], p["lru_b_a"], p["lru_w_x"], p["lru_b_x"],
           p["lru_lambda"]]))
    return jnp.stack(outs[:n_tok], axis=1).reshape(db * n_tok, w), outs[n_tok]


def _xattn_prompt_body(q_ref, k_ref, v_ref, o_ref):
    scale = HD ** -0.5
    for h in range(X_H):
        cs = slice(h * HD, (h + 1) * HD)
        q = (q_ref[:, cs] * scale).astype(BF16)
        s = lax.dot_general(q, k_ref[0, :, cs].astype(BF16), NT_DIMS, preferred_element_type=F32)
        p = jnp.exp(s - jnp.max(s, axis=-1, keepdims=True))
        l = jnp.sum(p, axis=-1, keepdims=True)
        o = jnp.dot(p.astype(BF16), v_ref[0, :, cs].astype(BF16), preferred_element_type=F32) / l
        o_ref[:, cs] = o.astype(o_ref.dtype)


def _xattn_prompt(q, mk, mv, nb, t):
    w = X_H * HD
    m = mk.shape[1]
    tq = min(512, t)
    return pl.pallas_call(
        _xattn_prompt_body,
        grid=(nb, t // tq),
        in_specs=[pl.BlockSpec((tq, w), lambda b, i: (b * (t // tq) + i, 0)),
                  pl.BlockSpec((1, m, w), lambda b, i: (b, 0, 0)),
                  pl.BlockSpec((1, m, w), lambda b, i: (b, 0, 0))],
        out_specs=pl.BlockSpec((tq, w), lambda b, i: (b * (t // tq) + i, 0)),
        out_shape=jax.ShapeDtypeStruct((nb * t, w), BF16),
        compiler_params=_cparams("parallel", "arbitrary"),
        name="xattn_prompt",
    )(q, mk, mv)


def _xattn_sample_body(q_ref, k_ref, v_ref, o_ref, *, n_tok, n_seq):
    scale = HD ** -0.5
    o_ref[...] = jnp.zeros_like(o_ref)
    for i in range(n_seq):
        qbd = _blockdiag_rows(q_ref[i] * scale, n_tok, X_H).astype(BF16)
        s = lax.dot_general(qbd, k_ref[0, i].astype(BF16), NT_DIMS, preferred_element_type=F32)
        p = jnp.exp(s - jnp.max(s, axis=-1, keepdims=True))
        l = jnp.sum(p, axis=-1, keepdims=True)
        acc = jnp.dot(p.astype(BF16), v_ref[0, i].astype(BF16), preferred_element_type=F32) / l
        rows = _blockdiag_fold(acc, n_tok, X_H)
        for t in range(n_tok):
            o_ref[i, t:t + 1, :] = rows[t].astype(o_ref.dtype)


def _xattn_sample(qs, cache_mk, cache_mv, layer, n_tok):
    db = qs.shape[0]
    w = X_H * HD
    m = cache_mk.shape[2]
    n_seq = min(8, db)
    kv = pl.BlockSpec((1, n_seq, m, w), lambda b: (layer, b, 0, 0))
    return pl.pallas_call(
        functools.partial(_xattn_sample_body, n_tok=n_tok, n_seq=n_seq),
        grid=(db // n_seq,),
        in_specs=[pl.BlockSpec((n_seq, PAD_L, w), lambda b: (b, 0, 0)), kv, kv],
        out_specs=pl.BlockSpec((n_seq, PAD_L, w), lambda b: (b, 0, 0)),
        out_shape=jax.ShapeDtypeStruct((db, PAD_L, w), BF16),
        compiler_params=_cparams("parallel"),
        name="xattn_sample",
    )(qs, cache_mk, cache_mv)


def _segment_offsets(d):
    fox_w, hg_w, lru_w = FOX_H * HD, HG_H * HD, HG_H * HD
    names = [("fq", fox_w), ("fk", fox_w), ("fv", fox_w), ("hq", hg_w), ("hf", hg_w), ("hi", hg_w), ("hg", hg_w),
             ("lx", lru_w), ("ly", lru_w), ("ga", d), ("gb", d), ("gc", d), ("ff", SEG)]
    offs, o = {}, 0
    for n, wdt in names:
        offs[n] = o
        o += wdt
    return offs, o


def _rearrange_w_in(w_in, d):
    fw = FOX_H * HD
    a, ff, b = w_in[:, :3 * fw], w_in[:, 3 * fw:3 * fw + FOX_H], w_in[:, 3 * fw + FOX_H:]
    return jnp.concatenate([a, b, ff, jnp.zeros((d, SEG - FOX_H), w_in.dtype)], axis=1).astype(BF16)


def _pad_tokens(x, db, n_tok):
    return jnp.pad(x.reshape(db, n_tok, x.shape[1]), ((0, 0), (0, PAD_L - n_tok), (0, 0)))


def kernel(x_prompt, x_sample, mem_prompt, cache_k, cache_v, cache_logf, state_hgrn, state_lru, state_conv, cache_mem_k, cache_mem_v, page_table, ffn1_norm, ffn1_w_gate, ffn1_w_up, ffn1_w_down, mix_norm, w_in, fox_b_f, fox_q_norm, fox_k_norm, hg_lb_param, hg_out_norm, lru_conv_w, lru_conv_b, lru_w_a, lru_b_a, lru_w_x, lru_b_x, lru_lambda, w_br_fox, w_br_hg, w_br_lru, w_out, xa_norm, xa_mem_norm, xa_w_q, xa_w_kv, xa_q_norm, xa_k_norm, xa_w_o, ffn2_norm, ffn2_w_gate, ffn2_w_up, ffn2_w_down):
    nb, t, d = x_prompt.shape
    db, n_tok, _ = x_sample.shape
    depth = w_in.shape[0]
    n_mem = mem_prompt.shape[1]
    np_rows = nb * t
    fw, xw = FOX_H * HD, X_H * HD
    lw = lru_conv_b.shape[1]
    assert n_tok <= PAD_L and n_tok >= CONV_W - 1 and t >= CONV_W - 1
    offs, nz = _segment_offsets(d)

    sm = jax.nn.softmax(hg_lb_param.astype(F32), axis=0)
    hg_lb = jnp.cumsum(sm, axis=0) - sm[0]

    x = jnp.concatenate([x_prompt.reshape(np_rows, d), x_sample.reshape(db * n_tok, d)], axis=0)
    mem = mem_prompt.reshape(nb * n_mem, d)
    cmk = cache_mem_k.reshape(depth, db, n_mem, xw)
    cmv = cache_mem_v.reshape(depth, db, n_mem, xw)

    outs = {n: [] for n in ("pk", "pv", "plf", "ps", "ph", "pc", "pmk", "pmv", "sk", "sv", "slf", "ss", "sh", "sc")}
    for l in range(depth):
        bf = lambda a: a[l].astype(BF16)
        row = lambda a: a[l].reshape(1, -1).astype(F32)
        lru_p = dict(lru_conv_w=lru_conv_w[l], lru_conv_b=row(lru_conv_b), lru_w_a=bf(lru_w_a), lru_b_a=row(lru_b_a),
                     lru_w_x=bf(lru_w_x), lru_b_x=row(lru_b_x), lru_lambda=row(lru_lambda))
        fox_bias = jnp.pad(row(fox_b_f), ((0, 0), (0, LANES - FOX_H)))
        head_gain = jnp.concatenate([jnp.tile(row(fox_q_norm), (1, FOX_H)), jnp.tile(row(fox_k_norm), (1, FOX_H)),
                                     jnp.ones((1, nz - 2 * fw), F32)], axis=1)

        x = _ffn(x, ffn1_norm[l], bf(ffn1_w_gate), bf(ffn1_w_up), bf(ffn1_w_down))
        z = _norm_matmul(x, mix_norm[l], _rearrange_w_in(w_in[l], d), head_gain, n_norm_cols=2 * fw)
        zs = _pad_tokens(z[np_rows:], db, n_tok)

        plf, c_row = _fox_prep(z, fox_bias, nb, t, offs["ff"])
        of_p = _fox_prompt(z, c_row, nb, t, offs)
        of_s, slf = _fox_sample(zs, fox_bias, cache_k, cache_v, cache_logf, page_table, l, offs, n_tok)
        oh_p, ps = _hg_prompt(z, row(hg_lb), row(hg_out_norm), nb, t, offs)
        oh_s, ss = _hg_sample(zs, row(hg_lb), row(hg_out_norm), state_hgrn, l, offs, n_tok)
        ol_p, ph = _lru_prompt(z, lru_p, nb, t, offs)
        ol_s, sh = _lru_sample(zs.reshape(db, PAD_L * nz), lru_p, state_lru, state_conv, l, offs, n_tok, nz)

        unpad = lambda a: a[:, :n_tok].reshape(db * n_tok, a.shape[2])
        o_fox = jnp.concatenate([of_p, unpad(of_s)], axis=0)
        o_hg = jnp.concatenate([oh_p, unpad(oh_s)], axis=0)
        o_lru = jnp.concatenate([ol_p, ol_s], axis=0)
        y = _merge(o_fox, o_hg, o_lru, bf(w_br_fox), bf(w_br_hg), bf(w_br_lru), z, offs["ga"])
        x = _matmul_residual(y, bf(w_out), x)

        xq = _norm_matmul(x, xa_norm[l], bf(xa_w_q), jnp.tile(row(xa_q_norm), (1, X_H)), n_norm_cols=xw)
        kv_gain = jnp.concatenate([jnp.tile(row(xa_k_norm), (1, X_H)), jnp.ones((1, xw), F32)], axis=1)
        mkv = _norm_matmul(mem, xa_mem_norm[l], bf(xa_w_kv), kv_gain, n_norm_cols=xw)
        mk, mv = mkv[:, :xw].reshape(nb, n_mem, xw), mkv[:, xw:].reshape(nb, n_mem, xw)
        ox_p = _xattn_prompt(xq[:np_rows], mk, mv, nb, t)
        ox_s = _xattn_sample(_pad_tokens(xq[np_rows:], db, n_tok), cmk, cmv, l, n_tok)
        x = _matmul_residual(jnp.concatenate([ox_p, unpad(ox_s)], axis=0), bf(xa_w_o), x)

        x = _ffn(x, ffn2_norm[l], bf(ffn2_w_gate), bf(ffn2_w_up), bf(ffn2_w_down))

        zp, zsm = z[:np_rows], z[np_rows:]
        outs["pk"].append(zp[:, offs["fk"]:offs["fk"] + fw].reshape(nb, t, FOX_H, HD))
        outs["pv"].append(zp[:, offs["fv"]:offs["fv"] + fw].reshape(nb, t, FOX_H, HD))
        outs["plf"].append(plf)
        outs["ps"].append(ps)
        outs["ph"].append(ph.reshape(nb, lw))
        outs["pc"].append(zp[:, offs["lx"]:offs["lx"] + lw].reshape(nb, t, lw)[:, t - (CONV_W - 1):])
        outs["pmk"].append(mk.reshape(nb, n_mem, X_H, HD))
        outs["pmv"].append(mv.reshape(nb, n_mem, X_H, HD))
        outs["sk"].append(zsm[:, offs["fk"]:offs["fk"] + fw].reshape(db, n_tok, FOX_H, HD))
        outs["sv"].append(zsm[:, offs["fv"]:offs["fv"] + fw].reshape(db, n_tok, FOX_H, HD))
        outs["slf"].append(slf[:, :n_tok, :FOX_H])
        outs["ss"].append(ss)
        outs["sh"].append(sh)
        outs["sc"].append(zsm[:, offs["lx"]:offs["lx"] + lw].reshape(db, n_tok, lw)[:, n_tok - (CONV_W - 1):])

    st = lambda n: jnp.stack(outs[n])
    return (x[:np_rows].reshape(nb, t, d), x[np_rows:].reshape(db, n_tok, d),
            st("pk"), st("pv"), st("plf"), st("ps"), st("ph"), st("pc"), st("pmk"), st("pmv"),
            st("sk"), st("sv"), st("slf"), st("ss"), st("sh"), st("sc"))
```

```python
import functools

import jax
import jax.numpy as jnp
from jax import lax
from jax.experimental import pallas as pl
from jax.experimental.pallas import tpu as pltpu

F32 = jnp.float32
BF16 = jnp.bfloat16
EPS = 1e-6
NEG_INF = -1e30
HD = 128
FOX_H = 8
HG_H = 8
X_H = 4
LRU_C = 8.0
CONV_W = 4
SUB = 8
LANES = 128
SEG = 512
FF_SEG = 1024
VMEM_LIMIT = 56 * 1024 * 1024
ROW_TILES = (768, 512, 256, 128, 64, 32, 16, 8)
PAD_L = 8
PAGE_GROUP = 8

NT_DIMS = (((1,), (1,)), ((), ()))
TN_DIMS = (((0,), (0,)), ((), ()))


def _cparams(*sem):
    return pltpu.CompilerParams(dimension_semantics=sem, vmem_limit_bytes=VMEM_LIMIT)


def _row_tile(n):
    return next(t for t in ROW_TILES if n % t == 0)


def _col_tile(n, cands, also_divides=0):
    return next(t for t in cands if n % t == 0 and also_divides % t == 0)


def _rms(x, g):
    return x * lax.rsqrt(jnp.mean(x * x, axis=-1, keepdims=True) + EPS) * g


def _sigmoid(x):
    return jax.nn.sigmoid(x)


def _silu(x):
    return x * jax.nn.sigmoid(x)


def _log_sigmoid(x):
    return jnp.minimum(x, 0.0) - jnp.log1p(jnp.exp(-jnp.abs(x)))


def _softplus(x):
    return jnp.maximum(x, 0.0) + jnp.log1p(jnp.exp(-jnp.abs(x)))


def _gelu_tanh(x):
    return 0.5 * x * (1.0 + jnp.tanh(0.7978845608028654 * (x + 0.044715 * (x * x * x))))


def _cumsum_lanes(x, reverse=False):
    lane = lax.broadcasted_iota(jnp.int32, x.shape, 1)
    d = 1
    while d < LANES:
        if reverse:
            x = x + jnp.where(lane < LANES - d, pltpu.roll(x, LANES - d, 1), 0.0)
        else:
            x = x + jnp.where(lane >= d, pltpu.roll(x, d, 1), 0.0)
        d *= 2
    return x


def _cumsum_rows(x):
    c = x.shape[0]
    row = lax.broadcasted_iota(jnp.int32, (c, 1), 0)
    d = 1
    while d < c:
        x = x + jnp.where(row >= d, pltpu.roll(x, d, 0), 0.0)
        d *= 2
    return x


def _row_to_col(r):
    return jnp.broadcast_to(r, (SUB, r.shape[1])).T[:, 0:1]


def _ffn_body(x_ref, g_ref, wg_ref, wu_ref, wd_ref, o_ref, h_ref, acc_ref):
    j = pl.program_id(1)

    @pl.when(j == 0)
    def _():
        h_ref[...] = _rms(x_ref[...], g_ref[...]).astype(BF16)
        acc_ref[...] = jnp.zeros_like(acc_ref)

    h = h_ref[...]
    gt = jnp.dot(h, wg_ref[...], preferred_element_type=F32)
    ut = jnp.dot(h, wu_ref[...], preferred_element_type=F32)
    a = (_silu(gt) * ut).astype(BF16)
    acc_ref[...] += jnp.dot(a, wd_ref[...], preferred_element_type=F32)

    @pl.when(j == pl.num_programs(1) - 1)
    def _():
        o_ref[...] = x_ref[...] + 0.5 * acc_ref[...]


def _ffn(x, g, wg, wu, wd, tf=512):
    n, d = x.shape
    f = wg.shape[1]
    tm = _row_tile(n)
    tf = min(tf, f)
    return pl.pallas_call(
        _ffn_body,
        grid=(n // tm, f // tf),
        in_specs=[
            pl.BlockSpec((tm, d), lambda i, j: (i, 0)),
            pl.BlockSpec((1, d), lambda i, j: (0, 0)),
            pl.BlockSpec((d, tf), lambda i, j: (0, j)),
            pl.BlockSpec((d, tf), lambda i, j: (0, j)),
            pl.BlockSpec((tf, d), lambda i, j: (j, 0)),
        ],
        out_specs=pl.BlockSpec((tm, d), lambda i, j: (i, 0)),
        out_shape=jax.ShapeDtypeStruct((n, d), F32),
        scratch_shapes=[pltpu.VMEM((tm, d), BF16), pltpu.VMEM((tm, d), F32)],
        compiler_params=_cparams("parallel", "arbitrary"),
        name="ffn",
    )(x, g.reshape(1, d), wg, wu, wd)


def _nmm_body(x_ref, g_ref, w_ref, gn_ref, o_ref, h_ref, *, n_norm_tiles):
    j = pl.program_id(1)

    @pl.when(j == 0)
    def _():
        h_ref[...] = _rms(x_ref[...], g_ref[...]).astype(BF16)

    y = jnp.dot(h_ref[...], w_ref[...], preferred_element_type=F32)
    if n_norm_tiles == 0:
        o_ref[...] = y
    else:
        @pl.when(j < n_norm_tiles)
        def _():
            tn = y.shape[1]
            parts = [_rms(y[:, c * HD:(c + 1) * HD], gn_ref[:, c * HD:(c + 1) * HD]) for c in range(tn // HD)]
            o_ref[...] = jnp.concatenate(parts, axis=1)

        @pl.when(j >= n_norm_tiles)
        def _():
            o_ref[...] = y


def _norm_matmul(x, g, w, head_gain=None, n_norm_cols=0):
    n, d = x.shape
    nout = w.shape[1]
    tm = _row_tile(n)
    tn = _col_tile(nout, (2048, 1024, 512, 256, 128), n_norm_cols)
    if head_gain is None:
        head_gain = jnp.ones((1, nout), F32)
    return pl.pallas_call(
        functools.partial(_nmm_body, n_norm_tiles=n_norm_cols // tn),
        grid=(n // tm, nout // tn),
        in_specs=[
            pl.BlockSpec((tm, d), lambda i, j: (i, 0)),
            pl.BlockSpec((1, d), lambda i, j: (0, 0)),
            pl.BlockSpec((d, tn), lambda i, j: (0, j)),
            pl.BlockSpec((1, tn), lambda i, j: (0, j)),
        ],
        out_specs=pl.BlockSpec((tm, tn), lambda i, j: (i, j)),
        out_shape=jax.ShapeDtypeStruct((n, nout), F32),
        scratch_shapes=[pltpu.VMEM((tm, d), BF16)],
        compiler_params=_cparams("parallel", "arbitrary"),
        name="norm_matmul",
    )(x, g.reshape(1, d), w, head_gain)


def _mmres_body(a_ref, w_ref, r_ref, o_ref):
    o_ref[...] = r_ref[...] + jnp.dot(a_ref[...], w_ref[...], preferred_element_type=F32)


def _matmul_residual(a, w, res, tn=SEG):
    n, k = a.shape
    nout = w.shape[1]
    tm = _row_tile(n)
    tn = min(tn, nout)
    return pl.pallas_call(
        _mmres_body,
        grid=(n // tm, nout // tn),
        in_specs=[
            pl.BlockSpec((tm, k), lambda i, j: (i, 0)),
            pl.BlockSpec((k, tn), lambda i, j: (0, j)),
            pl.BlockSpec((tm, tn), lambda i, j: (i, j)),
        ],
        out_specs=pl.BlockSpec((tm, tn), lambda i, j: (i, j)),
        out_shape=jax.ShapeDtypeStruct((n, nout), F32),
        compiler_params=_cparams("parallel", "arbitrary"),
        name="matmul_residual",
    )(a, w, res)


def _merge_body(of_ref, oh_ref, ol_ref, wf_ref, wh_ref, wl_ref, ga_ref, gb_ref, gc_ref, o_ref):
    y = _sigmoid(ga_ref[...]) * jnp.dot(of_ref[...], wf_ref[...], preferred_element_type=F32)
    y = y + _sigmoid(gb_ref[...]) * jnp.dot(oh_ref[...], wh_ref[...], preferred_element_type=F32)
    y = y + _sigmoid(gc_ref[...]) * jnp.dot(ol_ref[...], wl_ref[...], preferred_element_type=F32)
    o_ref[...] = y.astype(o_ref.dtype)


def _merge(o_fox, o_hg, o_lru, wf, wh, wl, z, gate_off, tn=SEG):
    n, k = o_fox.shape
    d = wf.shape[1]
    tm = _row_tile(n)
    tn = min(tn, d)
    gblk = gate_off // tn
    dblk = d // tn
    act = pl.BlockSpec((tm, k), lambda i, j: (i, 0))
    wsp = pl.BlockSpec((k, tn), lambda i, j: (0, j))
    return pl.pallas_call(
        _merge_body,
        grid=(n // tm, dblk),
        in_specs=[act, act, act, wsp, wsp, wsp,
                  pl.BlockSpec((tm, tn), lambda i, j: (i, gblk + j)),
                  pl.BlockSpec((tm, tn), lambda i, j: (i, gblk + dblk + j)),
                  pl.BlockSpec((tm, tn), lambda i, j: (i, gblk + 2 * dblk + j))],
        out_specs=pl.BlockSpec((tm, tn), lambda i, j: (i, j)),
        out_shape=jax.ShapeDtypeStruct((n, d), BF16),
        compiler_params=_cparams("parallel", "arbitrary"),
        name="merge",
    )(o_fox, o_hg, o_lru, wf, wh, wl, z, z, z)


def _fox_prep_body(zf_ref, bias_ref, lf_ref, c_ref):
    t = zf_ref.shape[0]
    lf = _log_sigmoid(zf_ref[:, 0:LANES] + bias_ref[...])
    lf_ref[0] = lf[:, 0:FOX_H]
    x = lf.T[0:FOX_H, :]
    carry = jnp.zeros((FOX_H, 1), F32)
    for blk in range(t // LANES):
        cs = _cumsum_lanes(x[:, blk * LANES:(blk + 1) * LANES]) + carry
        c_ref[0, :, blk * LANES:(blk + 1) * LANES] = cs
        carry = cs[:, LANES - 1:LANES]


def _fox_prep(z, bias, nb, t, ff_off):
    return pl.pallas_call(
        _fox_prep_body,
        grid=(nb,),
        in_specs=[pl.BlockSpec((t, SEG), lambda b: (b, ff_off // SEG)),
                  pl.BlockSpec((1, LANES), lambda b: (0, 0))],
        out_specs=[pl.BlockSpec((1, t, FOX_H), lambda b: (b, 0, 0)),
                   pl.BlockSpec((1, FOX_H, t), lambda b: (b, 0, 0))],
        out_shape=[jax.ShapeDtypeStruct((nb, t, FOX_H), F32),
                   jax.ShapeDtypeStruct((nb, FOX_H, t), F32)],
        compiler_params=_cparams("parallel"),
        name="fox_prep",
    )(z, bias)


def _fox_prompt_body(q_ref, k_ref, v_ref, c_ref, o_ref, *, tq):
    t = q_ref.shape[0]
    scale = HD ** -0.5
    c = c_ref[0, 0]
    ccol = _row_to_col(c)
    k = k_ref[...].astype(BF16)
    v = v_ref[...].astype(BF16)
    for i in range(t // tq):
        n_keys = (i + 1) * tq
        q = (q_ref[i * tq:(i + 1) * tq, :] * scale).astype(BF16)
        s = lax.dot_general(q, k[0:n_keys], NT_DIMS, preferred_element_type=F32)
        s = s + ccol[i * tq:(i + 1) * tq] - c[:, 0:n_keys]
        row = lax.broadcasted_iota(jnp.int32, s.shape, 0) + i * tq
        col = lax.broadcasted_iota(jnp.int32, s.shape, 1)
        s = jnp.where(col <= row, s, NEG_INF)
        m = jnp.max(s, axis=-1, keepdims=True)
        p = jnp.exp(s - m)
        l = jnp.sum(p, axis=-1, keepdims=True)
        o = jnp.dot(p.astype(BF16), v[0:n_keys], preferred_element_type=F32) / l
        o_ref[i * tq:(i + 1) * tq, :] = o.astype(o_ref.dtype)


def _fox_prompt(z, c_row, nb, t, offs):
    qb, kb, vb = (offs[n] // HD for n in ("fq", "fk", "fv"))
    tq = min(256, t)
    return pl.pallas_call(
        functools.partial(_fox_prompt_body, tq=tq),
        grid=(nb, FOX_H),
        in_specs=[pl.BlockSpec((t, HD), lambda b, h: (b, qb + h)),
                  pl.BlockSpec((t, HD), lambda b, h: (b, kb + h)),
                  pl.BlockSpec((t, HD), lambda b, h: (b, vb + h)),
                  pl.BlockSpec((1, 1, 1, t), lambda b, h: (b, h, 0, 0))],
        out_specs=pl.BlockSpec((t, HD), lambda b, h: (b, h)),
        out_shape=jax.ShapeDtypeStruct((nb * t, FOX_H * HD), BF16),
        compiler_params=_cparams("parallel", "parallel"),
        name="fox_prompt",
    )(z, z, z, c_row.reshape(nb, FOX_H, 1, t))


def _strided_roll_add(x, shift, mask=None):
    y = pltpu.roll(x, shift, 1)
    return x + (y if mask is None else jnp.where(mask, y, 0.0))


def _fox_cache_sums_body(lf_ref, o_ref):
    lf = lf_ref[0]
    n = lf.shape[1]
    lane = lax.broadcasted_iota(jnp.int32, lf.shape, 1)
    incl, tot = lf, lf
    d = FOX_H
    while d < n:
        incl = _strided_roll_add(incl, n - d, lane < n - d)
        tot = _strided_roll_add(tot, d)
        d *= 2
    o_ref[0, :, 0:n] = incl - lf
    o_ref[0, :, n:2 * n] = tot


def _fox_cache_sums(cache_logf):
    depth, n_pool, page, nh = cache_logf.shape
    n = page * nh
    blk = next(t for t in (256, 128, 64, 32, 16, 8, n_pool) if n_pool % t == 0)
    sums = pl.pallas_call(
        _fox_cache_sums_body,
        grid=(depth, n_pool // blk),
        in_specs=[pl.BlockSpec((1, blk, n), lambda l, i: (l, i, 0))],
        out_specs=pl.BlockSpec((1, blk, 2 * n), lambda l, i: (l, i, 0)),
        out_shape=jax.ShapeDtypeStruct((depth, n_pool, 2 * n), F32),
        compiler_params=_cparams("parallel", "parallel"),
        name="fox_cache_sums",
    )(cache_logf.reshape(depth, n_pool, n))
    return sums.reshape(depth, n_pool, 2, n)


def _fox_gates_body(zf_ref, bias_ref, lf_ref, cum_ref, *, n_tok):
    rows = zf_ref.shape[0] * PAD_L
    lf = _log_sigmoid(zf_ref[:, :, 0:LANES].reshape(rows, LANES) + bias_ref[...])
    lf_ref[...] = lf
    pos = lax.broadcasted_iota(jnp.int32, (rows, 1), 0) & (PAD_L - 1)
    cum = jnp.where(pos < n_tok, lf, 0.0)
    d = 1
    while d < PAD_L:
        cum = cum + jnp.where(pos >= d, pltpu.roll(cum, d, 0), 0.0)
        d *= 2
    cum_ref[...] = cum


def _fox_gates(zs, bias, ff_off, n_tok):
    db = zs.shape[0]
    out = jax.ShapeDtypeStruct((db * PAD_L, LANES), F32)
    return pl.pallas_call(
        functools.partial(_fox_gates_body, n_tok=n_tok),
        grid=(1,),
        in_specs=[pl.BlockSpec((db, PAD_L, SEG), lambda i: (0, 0, ff_off // SEG)),
                  pl.BlockSpec((1, LANES), lambda i: (0, 0))],
        out_specs=[pl.BlockSpec((db * PAD_L, LANES), lambda i: (0, 0))] * 2,
        out_shape=[out, out],
        compiler_params=_cparams("arbitrary"),
        name="fox_gates",
    )(zs, bias)


def _fox_sample_body(pt_ref, q_ref, kn_ref, vn_ref, ccol_ref, crow_ref, *rest, n_grp):
    kp, vp, sums = rest[0:n_grp], rest[n_grp:2 * n_grp], rest[2 * n_grp:3 * n_grp]
    o_ref, m_ref, l_ref, acc_ref, carry_ref = rest[3 * n_grp:]
    p = pl.program_id(1)
    q = (q_ref[0] * HD ** -0.5).astype(BF16)

    def same_head(shape):
        r = lax.broadcasted_iota(jnp.int32, shape, 0)
        c = lax.broadcasted_iota(jnp.int32, shape, 1)
        return (r & (FOX_H - 1)) == (c & (FOX_H - 1)), r, c

    @pl.when(p == 0)
    def _():
        s = lax.dot_general(q, kn_ref[0].astype(BF16), NT_DIMS, preferred_element_type=F32)
        s = s + ccol_ref[0] - crow_ref[0]
        ok, r, c = same_head(s.shape)
        s = jnp.where(ok & ((c >> 3) <= (r >> 3)), s, NEG_INF)
        m = jnp.max(s, axis=-1, keepdims=True)
        pn = jnp.exp(s - m)
        m_ref[...] = m
        l_ref[...] = jnp.sum(pn, axis=-1, keepdims=True)
        acc_ref[...] = jnp.dot(pn.astype(BF16), vn_ref[0].astype(BF16), preferred_element_type=F32)
        carry_ref[...] = jnp.zeros_like(carry_ref)

    ccol = ccol_ref[0]
    carry = carry_ref[...]
    scores = [None] * n_grp
    for j in reversed(range(n_grp)):
        k2 = kp[j][0, 0].reshape(-1, HD).astype(BF16)
        s = lax.dot_general(q, k2, NT_DIMS, preferred_element_type=F32)
        s = s + ccol + (sums[j][0, 0, 0:1, :] + carry)
        carry = carry + sums[j][0, 0, 1:2, :]
        scores[j] = jnp.where(same_head(s.shape)[0], s, NEG_INF)
    carry_ref[...] = carry
    m_old = m_ref[...]
    m_new = m_old
    for s in scores:
        m_new = jnp.maximum(m_new, jnp.max(s, axis=-1, keepdims=True))
    alpha = jnp.exp(m_old - m_new)
    l_run = alpha * l_ref[...]
    acc = alpha * acc_ref[...]
    for j in range(n_grp):
        pp = jnp.exp(scores[j] - m_new)
        l_run = l_run + jnp.sum(pp, axis=-1, keepdims=True)
        v2 = vp[j][0, 0].reshape(-1, HD).astype(BF16)
        acc = acc + jnp.dot(pp.astype(BF16), v2, preferred_element_type=F32)
    m_ref[...], l_ref[...], acc_ref[...] = m_new, l_run, acc

    @pl.when(p == pl.num_programs(1) - 1)
    def _():
        o_ref[0] = (acc / l_run).astype(o_ref.dtype)


def _fox_sample(q, kn, vn, cum_col, cum_row, cache_k, cache_v, cache_sums, page_table, layer):
    db, nrow, _ = q.shape
    n_pages = page_table.shape[1]
    page = cache_k.shape[2]
    n_grp = next(g for g in (PAGE_GROUP, 4, 2, 1) if n_pages % g == 0)
    n_flat = page * FOX_H

    def page_map(j):
        return lambda b, p, pt: (layer, pt[b * n_pages + n_pages - (p + 1) * n_grp + j], 0, 0, 0)

    def sums_map(j):
        return lambda b, p, pt: (layer, pt[b * n_pages + n_pages - (p + 1) * n_grp + j], 0, 0)

    seq = lambda r, c: pl.BlockSpec((1, r, c), lambda b, p, pt: (b, 0, 0))
    kv_specs = [pl.BlockSpec((1, 1, page, FOX_H, HD), page_map(j)) for j in range(n_grp)]
    grid_spec = pltpu.PrefetchScalarGridSpec(
        num_scalar_prefetch=1,
        grid=(db, n_pages // n_grp),
        in_specs=[seq(nrow, HD), seq(PAD_L * FOX_H, HD), seq(PAD_L * FOX_H, HD), seq(nrow, 1), seq(1, PAD_L * FOX_H)]
                 + kv_specs + kv_specs
                 + [pl.BlockSpec((1, 1, 2, n_flat), sums_map(j)) for j in range(n_grp)],
        out_specs=seq(nrow, HD),
        scratch_shapes=[pltpu.VMEM((nrow, 1), F32), pltpu.VMEM((nrow, 1), F32), pltpu.VMEM((nrow, HD), F32),
                        pltpu.VMEM((1, n_flat), F32)],
    )
    return pl.pallas_call(
        functools.partial(_fox_sample_body, n_grp=n_grp),
        grid_spec=grid_spec,
        out_shape=jax.ShapeDtypeStruct((db, nrow, HD), BF16),
        compiler_params=_cparams("parallel", "arbitrary"),
        name="fox_sample",
    )(page_table.reshape(-1), q, kn, vn, cum_col, cum_row, *([cache_k] * n_grp), *([cache_v] * n_grp),
      *([cache_sums] * n_grp))


def _hg_chunk(q, k, v, lf, s0, k_scr=None, b_scr=None):
    c = q.shape[0]
    if c > SUB:
        tri = (lax.broadcasted_iota(jnp.int32, (c, c), 1) <= lax.broadcasted_iota(jnp.int32, (c, c), 0)).astype(F32)
        b = jnp.dot(tri, lf, preferred_element_type=F32, precision=lax.Precision.HIGHEST)
    else:
        b = _cumsum_rows(lf)
    o = jnp.dot((q * jnp.exp(b)).astype(BF16), s0.astype(BF16), preferred_element_type=F32)
    vb = v.astype(BF16)

    row = lax.broadcasted_iota(jnp.int32, (c, 1), 0)
    a_mat = None
    half = SUB
    while 2 * half <= c:
        nblk = c // (2 * half)
        shift = (2 * half).bit_length() - 1
        split = jnp.broadcast_to(b.reshape(nblk, 2 * half, HD)[:, half - 1:half, :],
                                 (nblk, 2 * half, HD)).reshape(c, HD)
        right = (row & (2 * half - 1)) >= half
        qt = jnp.where(right, q * jnp.exp(b - split), 0.0).astype(BF16)
        kt = jnp.where(right, 0.0, k * jnp.exp(split - b)).astype(BF16)
        a_l = lax.dot_general(qt, kt, NT_DIMS, preferred_element_type=F32)
        ri = lax.broadcasted_iota(jnp.int32, (c, c), 0) >> shift
        ci = lax.broadcasted_iota(jnp.int32, (c, c), 1) >> shift
        a_l = jnp.where(ri == ci, a_l, 0.0)
        a_mat = a_l if a_mat is None else a_mat + a_l
        half *= 2

    if k_scr is not None:
        k_scr[...] = k
        b_scr[...] = b
        k_row = lambda r: k_scr[r:r + 1, :]
        b_row = lambda r: b_scr[r:r + 1, :]
    else:
        k_row = lambda r: k[r:r + 1, :]
        b_row = lambda r: b[r:r + 1, :]
    sub_row = lax.broadcasted_iota(jnp.int32, (SUB, 1), 0)
    if a_mat is None:
        ob = jnp.zeros((SUB, HD), F32)
        for s in range(SUB):
            a_s = jnp.sum(q * k_row(s) * jnp.exp(b - b_row(s)), axis=-1, keepdims=True)
            ob = ob + jnp.where(sub_row >= s, a_s, 0.0) * v[s:s + 1, :]
        o = o + ob
    else:
        lane = lax.broadcasted_iota(jnp.int32, (SUB, c), 1)
        blocks = []
        for blk in range(c // SUB):
            sl = slice(blk * SUB, (blk + 1) * SUB)
            qb_, bb_ = q[sl], b[sl]
            acc = jnp.zeros((SUB, c), F32)
            for s in range(SUB):
                r = blk * SUB + s
                a_s = jnp.sum(qb_ * k_row(r) * jnp.exp(bb_ - b_row(r)), axis=-1, keepdims=True)
                acc = jnp.where(lane == r, a_s, acc)
            blocks.append(jnp.where(lane <= sub_row + blk * SUB, acc, 0.0))
        a_mat = a_mat + jnp.concatenate(blocks, axis=0)
        o = o + jnp.dot(a_mat.astype(BF16), vb, preferred_element_type=F32)

    b_last = b[c - 1:c, :]
    kt = (k * jnp.exp(b_last - b)).astype(BF16)
    s_new = _row_to_col(jnp.exp(b_last)) * s0 + lax.dot_general(kt, vb, TN_DIMS, preferred_element_type=F32)
    return o, s_new


def _hg_gates(hq, hf, lb):
    sig = _sigmoid(hf)
    lf = jnp.log(lb + (1.0 - lb) * sig)
    k = (1.0 - lb) * _sigmoid(-hf)
    return _silu(hq), k, lf


def _hg_prompt_body(hq_ref, hf_ref, hi_ref, hg_ref, lb_ref, gn_ref, o_ref, s_ref, k_scr, b_scr, *, chunk):
    t = hq_ref.shape[0]
    n_heads = hq_ref.shape[1] // HD
    gn = gn_ref[...]

    def step(ci, states):
        rows = pl.ds(pl.multiple_of(ci * chunk, chunk), chunk)
        new_states = []
        for h in range(n_heads):
            cs = slice(h * HD, (h + 1) * HD)
            q, k, lf = _hg_gates(hq_ref[rows, cs], hf_ref[rows, cs], lb_ref[:, cs])
            o, s = _hg_chunk(q, k, hi_ref[rows, cs], lf, states[h], k_scr.at[h], b_scr.at[h])
            o_ref[rows, cs] = (_rms(o, gn) * _silu(hg_ref[rows, cs])).astype(o_ref.dtype)
            new_states.append(s)
        return tuple(new_states)

    states = lax.fori_loop(0, t // chunk, step, tuple(jnp.zeros((HD, HD), F32) for _ in range(n_heads)))
    for h in range(n_heads):
        s_ref[0, h] = states[h]


def _hg_prompt(z, lb, gn, nb, t, offs, heads_per_step=2):
    hw = heads_per_step * HD
    qb, fb, ib, gb = (offs[n] // hw for n in ("hq", "hf", "hi", "hg"))
    chunk = min(128, t)
    col = lambda base: pl.BlockSpec((t, hw), lambda b, h: (b, base + h))
    return pl.pallas_call(
        functools.partial(_hg_prompt_body, chunk=chunk),
        grid=(nb, HG_H // heads_per_step),
        in_specs=[col(qb), col(fb), col(ib), col(gb),
                  pl.BlockSpec((1, hw), lambda b, h: (0, h)),
                  pl.BlockSpec((1, HD), lambda b, h: (0, 0))],
        out_specs=[pl.BlockSpec((t, hw), lambda b, h: (b, h)),
                   pl.BlockSpec((1, heads_per_step, HD, HD), lambda b, h: (b, h, 0, 0))],
        out_shape=[jax.ShapeDtypeStruct((nb * t, HG_H * HD), BF16),
                   jax.ShapeDtypeStruct((nb, HG_H, HD, HD), F32)],
        scratch_shapes=[pltpu.VMEM((heads_per_step, chunk, HD), F32)] * 2,
        compiler_params=_cparams("parallel", "parallel"),
        name="hgrn2_prompt",
    )(z, z, z, z, lb, gn)


def _hg_sample_body(hq_ref, hf_ref, hi_ref, hg_ref, lb_ref, gn_ref, s0_ref, o_ref, s_ref, *, n_tok):
    valid = lax.broadcasted_iota(jnp.int32, (PAD_L, 1), 0) < n_tok
    gn = gn_ref[...]
    for h in range(HG_H):
        cs = slice(h * HD, (h + 1) * HD)
        q, k, lf = _hg_gates(hq_ref[0, :, cs], hf_ref[0, :, cs], lb_ref[:, cs])
        k = jnp.where(valid, k, 0.0)
        lf = jnp.where(valid, lf, 0.0)
        o, s = _hg_chunk(q, k, hi_ref[0, :, cs], lf, s0_ref[0, 0, h])
        s_ref[0, h] = s
        o_ref[0, :, cs] = (_rms(o, gn) * _silu(hg_ref[0, :, cs])).astype(o_ref.dtype)


def _hg_sample(zs, lb, gn, state, layer, offs, n_tok):
    db = zs.shape[0]
    w = HG_H * HD
    col = lambda name: pl.BlockSpec((1, PAD_L, w), lambda b: (b, 0, offs[name] // w))
    return pl.pallas_call(
        functools.partial(_hg_sample_body, n_tok=n_tok),
        grid=(db,),
        in_specs=[col("hq"), col("hf"), col("hi"), col("hg"),
                  pl.BlockSpec((1, w), lambda b: (0, 0)),
                  pl.BlockSpec((1, HD), lambda b: (0, 0)),
                  pl.BlockSpec((1, 1, HG_H, HD, HD), lambda b: (layer, b, 0, 0, 0))],
        out_specs=[pl.BlockSpec((1, PAD_L, w), lambda b: (b, 0, 0)),
                   pl.BlockSpec((1, HG_H, HD, HD), lambda b: (b, 0, 0, 0))],
        out_shape=[jax.ShapeDtypeStruct((db, PAD_L, w), BF16),
                   jax.ShapeDtypeStruct((db, HG_H, HD, HD), F32)],
        compiler_params=_cparams("parallel"),
        name="hgrn2_sample",
    )(zs, zs, zs, zs, lb, gn, state)


def _lru_gates(xc, wa_ref, ba, wx_ref, bx, lam):
    nblk = xc.shape[1] // HD
    xb = xc.astype(BF16)
    pa = [jnp.dot(xb[:, i * HD:(i + 1) * HD], wa_ref[i], preferred_element_type=F32) for i in range(nblk)]
    px = [jnp.dot(xb[:, i * HD:(i + 1) * HD], wx_ref[i], preferred_element_type=F32) for i in range(nblk)]
    r = _sigmoid(jnp.concatenate(pa, axis=1) + ba)
    ig = _sigmoid(jnp.concatenate(px, axis=1) + bx)
    log_a = -LRU_C * r * _softplus(-lam)
    a = jnp.exp(log_a)
    u = jnp.sqrt(-jnp.tanh(log_a) * (1.0 + a * a)) * (ig * xc)
    return a, u


def _lru_prompt_body(lx_ref, ly_ref, cw_ref, cb_ref, wa_ref, ba_ref, wx_ref, bx_ref, lam_ref,
                     o_ref, h_ref, *, rt):
    t, wt = lx_ref.shape
    cw = cw_ref[...]
    sub_row = lax.broadcasted_iota(jnp.int32, (SUB, 1), 0)
    pos = lax.broadcasted_iota(jnp.int32, (rt, 1), 0) & (SUB - 1)

    def tile(ti, carry):
        h, tail = carry
        rows = pl.ds(pl.multiple_of(ti * rt, rt), rt)
        x = lx_ref[rows, :]
        xc = cb_ref[...] + x * cw[CONV_W - 1:CONV_W, :]
        for d in range(1, CONV_W):
            xr = pltpu.roll(x, d, 0)
            head = jnp.where(sub_row < d, pltpu.roll(tail, d, 0), xr[0:SUB])
            xs = jnp.concatenate([head, xr[SUB:]], axis=0) if rt > SUB else head
            xc = xc + xs * cw[CONV_W - 1 - d:CONV_W - d, :]
        a, u = _lru_gates(xc, wa_ref, ba_ref[...], wx_ref, bx_ref[...], lam_ref[...])
        d = 1
        while d < SUB:
            keep = pos >= d
            u = a * jnp.where(keep, pltpu.roll(u, d, 0), 0.0) + u
            a = a * jnp.where(keep, pltpu.roll(a, d, 0), 1.0)
            d *= 2
        gelu = _gelu_tanh(ly_ref[rows, :])
        for g in range(rt // SUB):
            sl = slice(g * SUB, (g + 1) * SUB)
            hs = a[sl] * h + u[sl]
            h = hs[SUB - 1:SUB, :]
            o_ref[pl.ds(pl.multiple_of(ti * rt, rt) + g * SUB, SUB), :] = (hs * gelu[sl]).astype(o_ref.dtype)
        return h, x[rt - SUB:rt, :]

    h, _ = lax.fori_loop(0, t // rt, tile, (jnp.zeros((1, wt), F32), jnp.zeros((SUB, wt), F32)))
    h_ref[0] = h


def _lru_prompt(z, p, nb, t, offs):
    w = p["lru_conv_b"].shape[1]
    wt = SEG
    nblk = wt // HD
    rt = min(128, t)
    xb, yb = offs["lx"] // wt, offs["ly"] // wt
    vec = pl.BlockSpec((1, wt), lambda b, c: (0, c))
    gate_w = pl.BlockSpec((nblk, HD, HD), lambda b, c: (c, 0, 0))
    return pl.pallas_call(
        functools.partial(_lru_prompt_body, rt=rt),
        grid=(nb, w // wt),
        in_specs=[pl.BlockSpec((t, wt), lambda b, c: (b, xb + c)),
                  pl.BlockSpec((t, wt), lambda b, c: (b, yb + c)),
                  pl.BlockSpec((CONV_W, wt), lambda b, c: (0, c)),
                  vec, gate_w, vec, gate_w, vec, vec],
        out_specs=[pl.BlockSpec((t, wt), lambda b, c: (b, c)),
                   pl.BlockSpec((1, 1, wt), lambda b, c: (b, 0, c))],
        out_shape=[jax.ShapeDtypeStruct((nb * t, w), BF16),
                   jax.ShapeDtypeStruct((nb, 1, w), F32)],
        compiler_params=_cparams("parallel", "parallel"),
        name="lru_prompt",
    )(z, z, p["lru_conv_w"], p["lru_conv_b"], p["lru_w_a"], p["lru_b_a"], p["lru_w_x"], p["lru_b_x"],
      p["lru_lambda"])


def _lru_sample_body(*refs, n_tok):
    lx = refs[0:n_tok]
    ly = refs[n_tok:2 * n_tok]
    buf = refs[2 * n_tok:2 * n_tok + CONV_W - 1]
    h0_ref, cw_ref, cb_ref, wa_ref, ba_ref, wx_ref, bx_ref, lam_ref = refs[2 * n_tok + CONV_W - 1:2 * n_tok + CONV_W + 7]
    outs = refs[2 * n_tok + CONV_W + 7:]
    o_refs, h_ref = outs[0:n_tok], outs[n_tok]
    cw = cw_ref[...]
    xp = [r[0] for r in buf] + [r[...] for r in lx]
    h = h0_ref[0]
    for t in range(n_tok):
        xc = cb_ref[...]
        for j in range(CONV_W):
            xc = xc + xp[t + j] * cw[j:j + 1, :]
        a, u = _lru_gates(xc, wa_ref, ba_ref[...], wx_ref, bx_ref[...], lam_ref[...])
        h = a * h + u
        o_refs[t][...] = (h * _gelu_tanh(ly[t][...])).astype(o_refs[t].dtype)
    h_ref[...] = h


def _lru_sample(zs2, p, state_lru, state_conv, layer, offs, n_tok, nz):
    db = zs2.shape[0]
    w = p["lru_conv_b"].shape[1]
    wt = SEG
    nblk = wt // HD
    sc = state_conv.reshape(state_conv.shape[0], db, (CONV_W - 1) * w)
    tok = lambda name, l: pl.BlockSpec((db, wt), lambda c: (0, (l * nz + offs[name]) // wt + c))
    vec = pl.BlockSpec((1, wt), lambda c: (0, c))
    gate_w = pl.BlockSpec((nblk, HD, HD), lambda c: (c, 0, 0))
    in_specs = ([tok("lx", l) for l in range(n_tok)] + [tok("ly", l) for l in range(n_tok)]
                + [pl.BlockSpec((1, db, wt), lambda c, j=j: (layer, 0, j * (w // wt) + c)) for j in range(CONV_W - 1)]
                + [pl.BlockSpec((1, db, wt), lambda c: (layer, 0, c)),
                   pl.BlockSpec((CONV_W, wt), lambda c: (0, c)),
                   vec, gate_w, vec, gate_w, vec, vec])
    out_spec = pl.BlockSpec((db, wt), lambda c: (0, c))
    outs = pl.pallas_call(
        functools.partial(_lru_sample_body, n_tok=n_tok),
        grid=(w // wt,),
        in_specs=in_specs,
        out_specs=[out_spec] * (n_tok + 1),
        out_shape=[jax.ShapeDtypeStruct((db, w), BF16)] * n_tok + [jax.ShapeDtypeStruct((db, w), F32)],
        compiler_params=_cparams("parallel"),
        name="lru_sample",
    )(*([zs2] * (2 * n_tok) + [sc] * (CONV_W - 1)
        + [state_lru, p["lru_conv_w"], p["lru_conv_b"], p["lru_w_a"], p["lru_b_a"], p["lru_w_x"], p["lru_b_x"],
           p["lru_lambda"]]))
    return jnp.stack(outs[:n_tok], axis=1).reshape(db * n_tok, w), outs[n_tok]


def _xattn_prompt_body(q_ref, k_ref, v_ref, o_ref):
    scale = HD ** -0.5
    for h in range(X_H):
        cs = slice(h * HD, (h + 1) * HD)
        q = (q_ref[:, cs] * scale).astype(BF16)
        s = lax.dot_general(q, k_ref[0, :, cs].astype(BF16), NT_DIMS, preferred_element_type=F32)
        p = jnp.exp(s - jnp.max(s, axis=-1, keepdims=True))
        l = jnp.sum(p, axis=-1, keepdims=True)
        o = jnp.dot(p.astype(BF16), v_ref[0, :, cs].astype(BF16), preferred_element_type=F32) / l
        o_ref[:, cs] = o.astype(o_ref.dtype)


def _xattn_prompt(q, mk, mv, nb, t):
    w = X_H * HD
    m = mk.shape[1]
    tq = min(512, t)
    return pl.pallas_call(
        _xattn_prompt_body,
        grid=(nb, t // tq),
        in_specs=[pl.BlockSpec((tq, w), lambda b, i: (b * (t // tq) + i, 0)),
                  pl.BlockSpec((1, m, w), lambda b, i: (b, 0, 0)),
                  pl.BlockSpec((1, m, w), lambda b, i: (b, 0, 0))],
        out_specs=pl.BlockSpec((tq, w), lambda b, i: (b * (t // tq) + i, 0)),
        out_shape=jax.ShapeDtypeStruct((nb * t, w), BF16),
        compiler_params=_cparams("parallel", "arbitrary"),
        name="xattn_prompt",
    )(q, mk, mv)


def _xattn_sample_body(q_ref, k_ref, v_ref, o_ref, *, n_seq):
    scale = HD ** -0.5
    for i in range(n_seq):
        q = (q_ref[i] * scale).astype(BF16)
        s = lax.dot_general(q, k_ref[0, i].astype(BF16), NT_DIMS, preferred_element_type=F32)
        r = lax.broadcasted_iota(jnp.int32, s.shape, 0)
        c = lax.broadcasted_iota(jnp.int32, s.shape, 1)
        s = jnp.where((r & (X_H - 1)) == (c & (X_H - 1)), s, NEG_INF)
        p = jnp.exp(s - jnp.max(s, axis=-1, keepdims=True))
        l = jnp.sum(p, axis=-1, keepdims=True)
        o = jnp.dot(p.astype(BF16), v_ref[0, i].astype(BF16), preferred_element_type=F32) / l
        o_ref[i] = o.astype(o_ref.dtype)


def _xattn_sample(q, cache_mk, cache_mv, layer):
    db, nrow, _ = q.shape
    mx = cache_mk.shape[2]
    n_seq = min(8, db)
    kv = pl.BlockSpec((1, n_seq, mx, HD), lambda b: (layer, b, 0, 0))
    return pl.pallas_call(
        functools.partial(_xattn_sample_body, n_seq=n_seq),
        grid=(db // n_seq,),
        in_specs=[pl.BlockSpec((n_seq, nrow, HD), lambda b: (b, 0, 0)), kv, kv],
        out_specs=pl.BlockSpec((n_seq, nrow, HD), lambda b: (b, 0, 0)),
        out_shape=jax.ShapeDtypeStruct((db, nrow, HD), BF16),
        compiler_params=_cparams("parallel"),
        name="xattn_sample",
    )(q, cache_mk, cache_mv)


def _segment_offsets(d):
    fox_w, hg_w, lru_w = FOX_H * HD, HG_H * HD, HG_H * HD
    names = [("fq", fox_w), ("fk", fox_w), ("fv", fox_w), ("hq", hg_w), ("hf", hg_w), ("hi", hg_w), ("hg", hg_w),
             ("lx", lru_w), ("ly", lru_w), ("ga", d), ("gb", d), ("gc", d), ("ff", FF_SEG)]
    offs, o = {}, 0
    for n, wdt in names:
        offs[n] = o
        o += wdt
    return offs, o


def _rearrange_w_in(w_in, d):
    fw = FOX_H * HD
    a, ff, b = w_in[:, :3 * fw], w_in[:, 3 * fw:3 * fw + FOX_H], w_in[:, 3 * fw + FOX_H:]
    return jnp.concatenate([a, b, ff, jnp.zeros((d, FF_SEG - FOX_H), w_in.dtype)], axis=1).astype(BF16)


def _pad_tokens(x, db, n_tok):
    return jnp.pad(x.reshape(db, n_tok, x.shape[1]), ((0, 0), (0, PAD_L - n_tok), (0, 0)))


def kernel(x_prompt, x_sample, mem_prompt, cache_k, cache_v, cache_logf, state_hgrn, state_lru, state_conv, cache_mem_k, cache_mem_v, page_table, ffn1_norm, ffn1_w_gate, ffn1_w_up, ffn1_w_down, mix_norm, w_in, fox_b_f, fox_q_norm, fox_k_norm, hg_lb_param, hg_out_norm, lru_conv_w, lru_conv_b, lru_w_a, lru_b_a, lru_w_x, lru_b_x, lru_lambda, w_br_fox, w_br_hg, w_br_lru, w_out, xa_norm, xa_mem_norm, xa_w_q, xa_w_kv, xa_q_norm, xa_k_norm, xa_w_o, ffn2_norm, ffn2_w_gate, ffn2_w_up, ffn2_w_down):
    nb, t, d = x_prompt.shape
    db, n_tok, _ = x_sample.shape
    depth = w_in.shape[0]
    n_mem = mem_prompt.shape[1]
    np_rows = nb * t
    fw, xw = FOX_H * HD, X_H * HD
    lw = lru_conv_b.shape[1]
    assert n_tok <= PAD_L and n_tok >= CONV_W - 1 and t >= CONV_W - 1
    offs, nz = _segment_offsets(d)

    sm = jax.nn.softmax(hg_lb_param.astype(F32), axis=0)
    hg_lb = jnp.cumsum(sm, axis=0) - sm[0]

    x = jnp.concatenate([x_prompt.reshape(np_rows, d), x_sample.reshape(db * n_tok, d)], axis=0)
    mem = mem_prompt.reshape(nb * n_mem, d)
    cmk = cache_mem_k.reshape(depth, db, n_mem * X_H, HD)
    cmv = cache_mem_v.reshape(depth, db, n_mem * X_H, HD)

    cache_sums = _fox_cache_sums(cache_logf)

    outs = {n: [] for n in ("pk", "pv", "plf", "ps", "ph", "pc", "pmk", "pmv", "sk", "sv", "slf", "ss", "sh", "sc")}
    for l in range(depth):
        bf = lambda a: a[l].astype(BF16)
        row = lambda a: a[l].reshape(1, -1).astype(F32)
        lru_p = dict(lru_conv_w=lru_conv_w[l], lru_conv_b=row(lru_conv_b), lru_w_a=bf(lru_w_a), lru_b_a=row(lru_b_a),
                     lru_w_x=bf(lru_w_x), lru_b_x=row(lru_b_x), lru_lambda=row(lru_lambda))
        fox_bias = jnp.pad(row(fox_b_f), ((0, 0), (0, LANES - FOX_H)))
        head_gain = jnp.concatenate([jnp.tile(row(fox_q_norm), (1, FOX_H)), jnp.tile(row(fox_k_norm), (1, FOX_H)),
                                     jnp.ones((1, nz - 2 * fw), F32)], axis=1)

        x = _ffn(x, ffn1_norm[l], bf(ffn1_w_gate), bf(ffn1_w_up), bf(ffn1_w_down))
        z = _norm_matmul(x, mix_norm[l], _rearrange_w_in(w_in[l], d), head_gain, n_norm_cols=2 * fw)
        zs = _pad_tokens(z[np_rows:], db, n_tok)

        plf, c_row = _fox_prep(z, fox_bias, nb, t, offs["ff"])
        of_p = _fox_prompt(z, c_row, nb, t, offs)
        slf, cum = _fox_gates(zs, fox_bias, offs["ff"], n_tok)
        slf = slf.reshape(db, PAD_L, LANES)
        cum = cum.reshape(db, PAD_L, LANES)[:, :, :FOX_H]
        heads = lambda name, rows: zs[:, :rows, offs[name]:offs[name] + fw].reshape(db, rows * FOX_H, HD)
        of_s = _fox_sample(heads("fq", n_tok), heads("fk", PAD_L), heads("fv", PAD_L),
                           cum[:, :n_tok].reshape(db, n_tok * FOX_H, 1), cum.reshape(db, 1, PAD_L * FOX_H),
                           cache_k, cache_v, cache_sums, page_table, l).reshape(db * n_tok, fw)
        oh_p, ps = _hg_prompt(z, row(hg_lb), row(hg_out_norm), nb, t, offs)
        oh_s, ss = _hg_sample(zs, row(hg_lb), row(hg_out_norm), state_hgrn, l, offs, n_tok)
        ol_p, ph = _lru_prompt(z, lru_p, nb, t, offs)
        ol_s, sh = _lru_sample(zs.reshape(db, PAD_L * nz), lru_p, state_lru, state_conv, l, offs, n_tok, nz)

        unpad = lambda a: a[:, :n_tok].reshape(db * n_tok, a.shape[2])
        o_fox = jnp.concatenate([of_p, of_s], axis=0)
        o_hg = jnp.concatenate([oh_p, unpad(oh_s)], axis=0)
        o_lru = jnp.concatenate([ol_p, ol_s], axis=0)
        y = _merge(o_fox, o_hg, o_lru, bf(w_br_fox), bf(w_br_hg), bf(w_br_lru), z, offs["ga"])
        x = _matmul_residual(y, bf(w_out), x)

        xq = _norm_matmul(x, xa_norm[l], bf(xa_w_q), jnp.tile(row(xa_q_norm), (1, X_H)), n_norm_cols=xw)
        kv_gain = jnp.concatenate([jnp.tile(row(xa_k_norm), (1, X_H)), jnp.ones((1, xw), F32)], axis=1)
        mkv = _norm_matmul(mem, xa_mem_norm[l], bf(xa_w_kv), kv_gain, n_norm_cols=xw)
        mk, mv = mkv[:, :xw].reshape(nb, n_mem, xw), mkv[:, xw:].reshape(nb, n_mem, xw)
        ox_p = _xattn_prompt(xq[:np_rows], mk, mv, nb, t)
        ox_s = _xattn_sample(xq[np_rows:].reshape(db, n_tok * X_H, HD), cmk, cmv, l).reshape(db * n_tok, xw)
        x = _matmul_residual(jnp.concatenate([ox_p, ox_s], axis=0), bf(xa_w_o), x)

        x = _ffn(x, ffn2_norm[l], bf(ffn2_w_gate), bf(ffn2_w_up), bf(ffn2_w_down))

        zp, zsm = z[:np_rows], z[np_rows:]
        outs["pk"].append(zp[:, offs["fk"]:offs["fk"] + fw].reshape(nb, t, FOX_H, HD))
        outs["pv"].append(zp[:, offs["fv"]:offs["fv"] + fw].reshape(nb, t, FOX_H, HD))
        outs["plf"].append(plf)
        outs["ps"].append(ps)
        outs["ph"].append(ph.reshape(nb, lw))
        outs["pc"].append(zp[:, offs["lx"]:offs["lx"] + lw].reshape(nb, t, lw)[:, t - (CONV_W - 1):])
        outs["pmk"].append(mk.reshape(nb, n_mem, X_H, HD))
        outs["pmv"].append(mv.reshape(nb, n_mem, X_H, HD))
        outs["sk"].append(zsm[:, offs["fk"]:offs["fk"] + fw].reshape(db, n_tok, FOX_H, HD))
        outs["sv"].append(zsm[:, offs["fv"]:offs["fv"] + fw].reshape(db, n_tok, FOX_H, HD))
        outs["slf"].append(slf[:, :n_tok, :FOX_H])
        outs["ss"].append(ss)
        outs["sh"].append(sh)
        outs["sc"].append(zsm[:, offs["lx"]:offs["lx"] + lw].reshape(db, n_tok, lw)[:, n_tok - (CONV_W - 1):])

    st = lambda n: jnp.stack(outs[n])
    return (x[:np_rows].reshape(nb, t, d), x[np_rows:].reshape(db, n_tok, d),
            st("pk"), st("pv"), st("plf"), st("ps"), st("ph"), st("pc"), st("pmk"), st("pmv"),
            st("sk"), st("sv"), st("slf"), st("ss"), st("sh"), st("sc"))
```

```python
import functools

import jax
import jax.numpy as jnp
from jax import lax
from jax.experimental import pallas as pl
from jax.experimental.pallas import tpu as pltpu

F32 = jnp.float32
BF16 = jnp.bfloat16
EPS = 1e-6
NEG_INF = -1e30
HD = 128
FOX_H = 8
HG_H = 8
X_H = 4
LRU_C = 8.0
CONV_W = 4
SUB = 8
LANES = 128
SEG = 512
FF_SEG = 1024
VMEM_LIMIT = 56 * 1024 * 1024
ROW_TILES = (768, 512, 256, 128, 64, 32, 16, 8)
PAD_L = 8
PAGE_GROUP = 8

NT_DIMS = (((1,), (1,)), ((), ()))
TN_DIMS = (((0,), (0,)), ((), ()))


def _cparams(*sem):
    return pltpu.CompilerParams(dimension_semantics=sem, vmem_limit_bytes=VMEM_LIMIT)


def _row_tile(n, cap=ROW_TILES[0]):
    return next(t for t in ROW_TILES if t <= cap and n % t == 0)


def _col_tile(n, cands, also_divides=0):
    return next(t for t in cands if n % t == 0 and also_divides % t == 0)


def _rms(x, g):
    return x * lax.rsqrt(jnp.mean(x * x, axis=-1, keepdims=True) + EPS) * g


def _sigmoid(x):
    return jax.nn.sigmoid(x)


def _silu(x):
    return x * jax.nn.sigmoid(x)


def _log_sigmoid(x):
    return jnp.minimum(x, 0.0) - jnp.log1p(jnp.exp(-jnp.abs(x)))


def _softplus(x):
    return jnp.maximum(x, 0.0) + jnp.log1p(jnp.exp(-jnp.abs(x)))


def _gelu_tanh(x):
    return 0.5 * x * (1.0 + jnp.tanh(0.7978845608028654 * (x + 0.044715 * (x * x * x))))


def _cumsum_lanes(x, reverse=False):
    lane = lax.broadcasted_iota(jnp.int32, x.shape, 1)
    d = 1
    while d < LANES:
        if reverse:
            x = x + jnp.where(lane < LANES - d, pltpu.roll(x, LANES - d, 1), 0.0)
        else:
            x = x + jnp.where(lane >= d, pltpu.roll(x, d, 1), 0.0)
        d *= 2
    return x


def _cumsum_rows(x):
    c = x.shape[0]
    row = lax.broadcasted_iota(jnp.int32, (c, 1), 0)
    d = 1
    while d < c:
        x = x + jnp.where(row >= d, pltpu.roll(x, d, 0), 0.0)
        d *= 2
    return x


def _row_to_col(r):
    return jnp.broadcast_to(r, (SUB, r.shape[1])).T[:, 0:1]


def _ffn_body(x_ref, g_ref, wg_ref, wu_ref, wd_ref, o_ref, h_ref, acc_ref):
    j = pl.program_id(1)

    @pl.when(j == 0)
    def _():
        h_ref[...] = _rms(x_ref[...], g_ref[...]).astype(BF16)
        acc_ref[...] = jnp.zeros_like(acc_ref)

    h = h_ref[...]
    gt = jnp.dot(h, wg_ref[...], preferred_element_type=F32)
    ut = jnp.dot(h, wu_ref[...], preferred_element_type=F32)
    a = (_silu(gt) * ut).astype(BF16)
    acc_ref[...] += jnp.dot(a, wd_ref[...], preferred_element_type=F32)

    @pl.when(j == pl.num_programs(1) - 1)
    def _():
        o_ref[...] = x_ref[...] + 0.5 * acc_ref[...]


def _ffn(x, g, wg, wu, wd, layer, tf=512):
    n, d = x.shape
    f = wg.shape[2]
    tm = _row_tile(n)
    tf = min(tf, f)
    return pl.pallas_call(
        _ffn_body,
        grid=(n // tm, f // tf),
        in_specs=[
            pl.BlockSpec((tm, d), lambda i, j: (i, 0)),
            pl.BlockSpec((1, d), lambda i, j: (0, 0)),
            pl.BlockSpec((None, d, tf), lambda i, j: (layer, 0, j)),
            pl.BlockSpec((None, d, tf), lambda i, j: (layer, 0, j)),
            pl.BlockSpec((None, tf, d), lambda i, j: (layer, j, 0)),
        ],
        out_specs=pl.BlockSpec((tm, d), lambda i, j: (i, 0)),
        out_shape=jax.ShapeDtypeStruct((n, d), F32),
        scratch_shapes=[pltpu.VMEM((tm, d), BF16), pltpu.VMEM((tm, d), F32)],
        compiler_params=_cparams("parallel", "arbitrary"),
        name="ffn",
    )(x, g.reshape(1, d), wg, wu, wd)


def _nmm_body(x_ref, g_ref, w_ref, gn_ref, o_ref, h_ref, *, n_norm_tiles):
    j = pl.program_id(1)

    @pl.when(j == 0)
    def _():
        h_ref[...] = _rms(x_ref[...], g_ref[...]).astype(BF16)

    y = jnp.dot(h_ref[...], w_ref[...], preferred_element_type=F32)
    if n_norm_tiles == 0:
        o_ref[...] = y
    else:
        @pl.when(j < n_norm_tiles)
        def _():
            tn = y.shape[1]
            parts = [_rms(y[:, c * HD:(c + 1) * HD], gn_ref[:, c * HD:(c + 1) * HD]) for c in range(tn // HD)]
            o_ref[...] = jnp.concatenate(parts, axis=1)

        @pl.when(j >= n_norm_tiles)
        def _():
            o_ref[...] = y


def _norm_matmul(x, g, w, layer, head_gain=None, n_norm_cols=0):
    n, d = x.shape
    nout = w.shape[2]
    tm = _row_tile(n)
    tn = _col_tile(nout, (2048, 1024, 512, 256, 128), n_norm_cols)
    if head_gain is None:
        head_gain = jnp.ones((1, nout), F32)
    return pl.pallas_call(
        functools.partial(_nmm_body, n_norm_tiles=n_norm_cols // tn),
        grid=(n // tm, nout // tn),
        in_specs=[
            pl.BlockSpec((tm, d), lambda i, j: (i, 0)),
            pl.BlockSpec((1, d), lambda i, j: (0, 0)),
            pl.BlockSpec((None, d, tn), lambda i, j: (layer, 0, j)),
            pl.BlockSpec((1, tn), lambda i, j: (0, j)),
        ],
        out_specs=pl.BlockSpec((tm, tn), lambda i, j: (i, j)),
        out_shape=jax.ShapeDtypeStruct((n, nout), F32),
        scratch_shapes=[pltpu.VMEM((tm, d), BF16)],
        compiler_params=_cparams("parallel", "arbitrary"),
        name="norm_matmul",
    )(x, g.reshape(1, d), w, head_gain)


def _mmres_body(a_ref, w_ref, r_ref, o_ref):
    o_ref[...] = r_ref[...] + jnp.dot(a_ref[...], w_ref[...], preferred_element_type=F32)


def _matmul_residual(a, w, res, layer):
    n, k = a.shape
    nout = w.shape[2]
    tm = _row_tile(n)
    tn = _col_tile(nout, (2048, 1024, 512, 256, 128))
    return pl.pallas_call(
        _mmres_body,
        grid=(n // tm, nout // tn),
        in_specs=[
            pl.BlockSpec((tm, k), lambda i, j: (i, 0)),
            pl.BlockSpec((None, k, tn), lambda i, j: (layer, 0, j)),
            pl.BlockSpec((tm, tn), lambda i, j: (i, j)),
        ],
        out_specs=pl.BlockSpec((tm, tn), lambda i, j: (i, j)),
        out_shape=jax.ShapeDtypeStruct((n, nout), F32),
        compiler_params=_cparams("parallel", "arbitrary"),
        name="matmul_residual",
    )(a, w, res)


def _merge_body(of_ref, oh_ref, ol_ref, wf_ref, wh_ref, wl_ref, ga_ref, gb_ref, gc_ref, wo_ref, x_ref, o_ref):
    j = pl.program_id(1)
    y = _sigmoid(ga_ref[...]) * jnp.dot(of_ref[...], wf_ref[...], preferred_element_type=F32)
    y = y + _sigmoid(gb_ref[...]) * jnp.dot(oh_ref[...], wh_ref[...], preferred_element_type=F32)
    y = y + _sigmoid(gc_ref[...]) * jnp.dot(ol_ref[...], wl_ref[...], preferred_element_type=F32)
    part = jnp.dot(y.astype(BF16), wo_ref[...], preferred_element_type=F32)

    @pl.when(j == 0)
    def _():
        o_ref[...] = x_ref[...] + part

    @pl.when(j > 0)
    def _():
        o_ref[...] += part


def _merge(o_fox, o_hg, o_lru, wf, wh, wl, z, gate_off, w_out, x, layer, tn=SEG):
    n, k = o_fox.shape
    d = wf.shape[2]
    tm = _row_tile(n, cap=512)
    tn = min(tn, d)
    gblk = gate_off // tn
    dblk = d // tn
    act = pl.BlockSpec((tm, k), lambda i, j: (i, 0))
    wsp = pl.BlockSpec((None, k, tn), lambda i, j: (layer, 0, j))
    full = pl.BlockSpec((tm, d), lambda i, j: (i, 0))
    return pl.pallas_call(
        _merge_body,
        grid=(n // tm, dblk),
        in_specs=[act, act, act, wsp, wsp, wsp,
                  pl.BlockSpec((tm, tn), lambda i, j: (i, gblk + j)),
                  pl.BlockSpec((tm, tn), lambda i, j: (i, gblk + dblk + j)),
                  pl.BlockSpec((tm, tn), lambda i, j: (i, gblk + 2 * dblk + j)),
                  pl.BlockSpec((None, tn, d), lambda i, j: (layer, j, 0)),
                  full],
        out_specs=full,
        out_shape=jax.ShapeDtypeStruct((n, d), F32),
        compiler_params=_cparams("parallel", "arbitrary"),
        name="merge",
    )(o_fox, o_hg, o_lru, wf, wh, wl, z, z, z, w_out, x)


def _fox_prep_body(zf_ref, bias_ref, lf_ref, c_ref):
    t = zf_ref.shape[0]
    lf = _log_sigmoid(zf_ref[:, 0:LANES] + bias_ref[...])
    lf_ref[0] = lf[:, 0:FOX_H]
    x = lf.T[0:FOX_H, :]
    carry = jnp.zeros((FOX_H, 1), F32)
    for blk in range(t // LANES):
        cs = _cumsum_lanes(x[:, blk * LANES:(blk + 1) * LANES]) + carry
        c_ref[0, :, blk * LANES:(blk + 1) * LANES] = cs
        carry = cs[:, LANES - 1:LANES]


def _fox_prep(z, bias, nb, t, ff_off):
    return pl.pallas_call(
        _fox_prep_body,
        grid=(nb,),
        in_specs=[pl.BlockSpec((t, SEG), lambda b: (b, ff_off // SEG)),
                  pl.BlockSpec((1, LANES), lambda b: (0, 0))],
        out_specs=[pl.BlockSpec((1, t, FOX_H), lambda b: (b, 0, 0)),
                   pl.BlockSpec((1, FOX_H, t), lambda b: (b, 0, 0))],
        out_shape=[jax.ShapeDtypeStruct((nb, t, FOX_H), F32),
                   jax.ShapeDtypeStruct((nb, FOX_H, t), F32)],
        compiler_params=_cparams("parallel"),
        name="fox_prep",
    )(z, bias)


def _fox_prompt_body(q_ref, k_ref, v_ref, c_ref, o_ref, *, tq):
    t = q_ref.shape[0]
    scale = HD ** -0.5
    c = c_ref[0, 0]
    ccol = _row_to_col(c)
    k = k_ref[...].astype(BF16)
    v = v_ref[...].astype(BF16)
    row = lax.broadcasted_iota(jnp.int32, (tq, tq), 0)
    col = lax.broadcasted_iota(jnp.int32, (tq, tq), 1)
    for i in range(t // tq):
        lo, hi = i * tq, (i + 1) * tq
        q = (q_ref[lo:hi, :] * scale).astype(BF16)
        cq = ccol[lo:hi]
        sd = lax.dot_general(q, k[lo:hi], NT_DIMS, preferred_element_type=F32) + cq - c[:, lo:hi]
        sd = jnp.where(col <= row, sd, NEG_INF)
        m = jnp.max(sd, axis=-1, keepdims=True)
        if i > 0:
            sp = lax.dot_general(q, k[0:lo], NT_DIMS, preferred_element_type=F32) + cq - c[:, 0:lo]
            m = jnp.maximum(m, jnp.max(sp, axis=-1, keepdims=True))
        pd = jnp.exp(sd - m)
        l = jnp.sum(pd, axis=-1, keepdims=True)
        o = jnp.dot(pd.astype(BF16), v[lo:hi], preferred_element_type=F32)
        if i > 0:
            pp = jnp.exp(sp - m)
            l = l + jnp.sum(pp, axis=-1, keepdims=True)
            o = o + jnp.dot(pp.astype(BF16), v[0:lo], preferred_element_type=F32)
        o_ref[lo:hi, :] = (o / l).astype(o_ref.dtype)


def _fox_prompt(z, c_row, nb, t, offs):
    qb, kb, vb = (offs[n] // HD for n in ("fq", "fk", "fv"))
    tq = min(256, t)
    return pl.pallas_call(
        functools.partial(_fox_prompt_body, tq=tq),
        grid=(nb, FOX_H),
        in_specs=[pl.BlockSpec((t, HD), lambda b, h: (b, qb + h)),
                  pl.BlockSpec((t, HD), lambda b, h: (b, kb + h)),
                  pl.BlockSpec((t, HD), lambda b, h: (b, vb + h)),
                  pl.BlockSpec((1, 1, 1, t), lambda b, h: (b, h, 0, 0))],
        out_specs=pl.BlockSpec((t, HD), lambda b, h: (b, h)),
        out_shape=jax.ShapeDtypeStruct((nb * t, FOX_H * HD), BF16),
        compiler_params=_cparams("parallel", "parallel"),
        name="fox_prompt",
    )(z, z, z, c_row.reshape(nb, FOX_H, 1, t))


def _strided_roll_add(x, shift, mask=None):
    y = pltpu.roll(x, shift, 1)
    return x + (y if mask is None else jnp.where(mask, y, 0.0))


def _fox_cache_sums_body(lf_ref, o_ref):
    lf = lf_ref[0]
    n = lf.shape[1]
    lane = lax.broadcasted_iota(jnp.int32, lf.shape, 1)
    incl, tot = lf, lf
    d = FOX_H
    while d < n:
        incl = _strided_roll_add(incl, n - d, lane < n - d)
        tot = _strided_roll_add(tot, d)
        d *= 2
    o_ref[0, :, 0:n] = incl - lf
    o_ref[0, :, n:2 * n] = tot


def _fox_cache_sums(cache_logf):
    depth, n_pool, page, nh = cache_logf.shape
    n = page * nh
    blk = next(t for t in (256, 128, 64, 32, 16, 8, n_pool) if n_pool % t == 0)
    sums = pl.pallas_call(
        _fox_cache_sums_body,
        grid=(depth, n_pool // blk),
        in_specs=[pl.BlockSpec((1, blk, n), lambda l, i: (l, i, 0))],
        out_specs=pl.BlockSpec((1, blk, 2 * n), lambda l, i: (l, i, 0)),
        out_shape=jax.ShapeDtypeStruct((depth, n_pool, 2 * n), F32),
        compiler_params=_cparams("parallel", "parallel"),
        name="fox_cache_sums",
    )(cache_logf.reshape(depth, n_pool, n))
    return sums.reshape(depth, n_pool, 2, n)


def _fox_gates_body(zf_ref, bias_ref, lf_ref, cum_ref, *, n_tok):
    rows = zf_ref.shape[0] * PAD_L
    lf = _log_sigmoid(zf_ref[:, :, 0:LANES].reshape(rows, LANES) + bias_ref[...])
    lf_ref[...] = lf
    pos = lax.broadcasted_iota(jnp.int32, (rows, 1), 0) & (PAD_L - 1)
    cum = jnp.where(pos < n_tok, lf, 0.0)
    d = 1
    while d < PAD_L:
        cum = cum + jnp.where(pos >= d, pltpu.roll(cum, d, 0), 0.0)
        d *= 2
    cum_ref[...] = cum


def _fox_gates(zs, bias, ff_off, n_tok):
    db = zs.shape[0]
    out = jax.ShapeDtypeStruct((db * PAD_L, LANES), F32)
    return pl.pallas_call(
        functools.partial(_fox_gates_body, n_tok=n_tok),
        grid=(1,),
        in_specs=[pl.BlockSpec((db, PAD_L, SEG), lambda i: (0, 0, ff_off // SEG)),
                  pl.BlockSpec((1, LANES), lambda i: (0, 0))],
        out_specs=[pl.BlockSpec((db * PAD_L, LANES), lambda i: (0, 0))] * 2,
        out_shape=[out, out],
        compiler_params=_cparams("arbitrary"),
        name="fox_gates",
    )(zs, bias)


def _fox_sample_body(pt_ref, q_ref, kn_ref, vn_ref, ccol_ref, crow_ref, *rest, n_grp):
    kp, vp, sums = rest[0:n_grp], rest[n_grp:2 * n_grp], rest[2 * n_grp:3 * n_grp]
    o_ref, m_ref, l_ref, acc_ref, carry_ref = rest[3 * n_grp:]
    p = pl.program_id(1)
    q = (q_ref[0] * HD ** -0.5).astype(BF16)

    def same_head(shape):
        r = lax.broadcasted_iota(jnp.int32, shape, 0)
        c = lax.broadcasted_iota(jnp.int32, shape, 1)
        return (r & (FOX_H - 1)) == (c & (FOX_H - 1)), r, c

    @pl.when(p == 0)
    def _():
        s = lax.dot_general(q, kn_ref[0].astype(BF16), NT_DIMS, preferred_element_type=F32)
        s = s + ccol_ref[0] - crow_ref[0]
        ok, r, c = same_head(s.shape)
        s = jnp.where(ok & ((c >> 3) <= (r >> 3)), s, NEG_INF)
        m = jnp.max(s, axis=-1, keepdims=True)
        pn = jnp.exp(s - m)
        m_ref[...] = m
        l_ref[...] = jnp.sum(pn, axis=-1, keepdims=True)
        acc_ref[...] = jnp.dot(pn.astype(BF16), vn_ref[0].astype(BF16), preferred_element_type=F32)
        carry_ref[...] = jnp.zeros_like(carry_ref)

    ccol = ccol_ref[0]
    carry = carry_ref[...]
    scores = [None] * n_grp
    for j in reversed(range(n_grp)):
        k2 = kp[j][0, 0].reshape(-1, HD).astype(BF16)
        s = lax.dot_general(q, k2, NT_DIMS, preferred_element_type=F32)
        s = s + ccol + (sums[j][0, 0, 0:1, :] + carry)
        carry = carry + sums[j][0, 0, 1:2, :]
        scores[j] = jnp.where(same_head(s.shape)[0], s, NEG_INF)
    carry_ref[...] = carry
    m_old = m_ref[...]
    m_new = m_old
    for s in scores:
        m_new = jnp.maximum(m_new, jnp.max(s, axis=-1, keepdims=True))
    alpha = jnp.exp(m_old - m_new)
    l_run = alpha * l_ref[...]
    acc = alpha * acc_ref[...]
    for j in range(n_grp):
        pp = jnp.exp(scores[j] - m_new)
        l_run = l_run + jnp.sum(pp, axis=-1, keepdims=True)
        v2 = vp[j][0, 0].reshape(-1, HD).astype(BF16)
        acc = acc + jnp.dot(pp.astype(BF16), v2, preferred_element_type=F32)
    m_ref[...], l_ref[...], acc_ref[...] = m_new, l_run, acc

    @pl.when(p == pl.num_programs(1) - 1)
    def _():
        o_ref[0] = (acc / l_run).astype(o_ref.dtype)


def _fox_sample(q, kn, vn, cum_col, cum_row, cache_k, cache_v, cache_sums, page_table, layer):
    db, nrow, _ = q.shape
    n_pages = page_table.shape[1]
    page = cache_k.shape[2]
    n_grp = next(g for g in (PAGE_GROUP, 4, 2, 1) if n_pages % g == 0)
    n_flat = page * FOX_H

    def page_map(j):
        return lambda b, p, pt: (layer, pt[b * n_pages + n_pages - (p + 1) * n_grp + j], 0, 0, 0)

    def sums_map(j):
        return lambda b, p, pt: (layer, pt[b * n_pages + n_pages - (p + 1) * n_grp + j], 0, 0)

    seq = lambda r, c: pl.BlockSpec((1, r, c), lambda b, p, pt: (b, 0, 0))
    kv_specs = [pl.BlockSpec((1, 1, page, FOX_H, HD), page_map(j)) for j in range(n_grp)]
    grid_spec = pltpu.PrefetchScalarGridSpec(
        num_scalar_prefetch=1,
        grid=(db, n_pages // n_grp),
        in_specs=[seq(nrow, HD), seq(PAD_L * FOX_H, HD), seq(PAD_L * FOX_H, HD), seq(nrow, 1), seq(1, PAD_L * FOX_H)]
                 + kv_specs + kv_specs
                 + [pl.BlockSpec((1, 1, 2, n_flat), sums_map(j)) for j in range(n_grp)],
        out_specs=seq(nrow, HD),
        scratch_shapes=[pltpu.VMEM((nrow, 1), F32), pltpu.VMEM((nrow, 1), F32), pltpu.VMEM((nrow, HD), F32),
                        pltpu.VMEM((1, n_flat), F32)],
    )
    return pl.pallas_call(
        functools.partial(_fox_sample_body, n_grp=n_grp),
        grid_spec=grid_spec,
        out_shape=jax.ShapeDtypeStruct((db, nrow, HD), BF16),
        compiler_params=_cparams("parallel", "arbitrary"),
        name="fox_sample",
    )(page_table.reshape(-1), q, kn, vn, cum_col, cum_row, *([cache_k] * n_grp), *([cache_v] * n_grp),
      *([cache_sums] * n_grp))


def _hg_chunk(q, k, v, lf, s0, k_scr=None, b_scr=None):
    c = q.shape[0]
    if c > SUB:
        tri = (lax.broadcasted_iota(jnp.int32, (c, c), 1) <= lax.broadcasted_iota(jnp.int32, (c, c), 0)).astype(F32)
        b = jnp.dot(tri, lf, preferred_element_type=F32, precision=lax.Precision.HIGHEST)
    else:
        b = _cumsum_rows(lf)
    o = jnp.dot((q * jnp.exp(b)).astype(BF16), s0.astype(BF16), preferred_element_type=F32)
    vb = v.astype(BF16)

    row = lax.broadcasted_iota(jnp.int32, (c, 1), 0)
    a_mat = None
    half = SUB
    while 2 * half <= c:
        nblk = c // (2 * half)
        shift = (2 * half).bit_length() - 1
        split = jnp.broadcast_to(b.reshape(nblk, 2 * half, HD)[:, half - 1:half, :],
                                 (nblk, 2 * half, HD)).reshape(c, HD)
        right = (row & (2 * half - 1)) >= half
        qt = jnp.where(right, q * jnp.exp(b - split), 0.0).astype(BF16)
        kt = jnp.where(right, 0.0, k * jnp.exp(split - b)).astype(BF16)
        a_l = lax.dot_general(qt, kt, NT_DIMS, preferred_element_type=F32)
        ri = lax.broadcasted_iota(jnp.int32, (c, c), 0) >> shift
        ci = lax.broadcasted_iota(jnp.int32, (c, c), 1) >> shift
        a_l = jnp.where(ri == ci, a_l, 0.0)
        a_mat = a_l if a_mat is None else a_mat + a_l
        half *= 2

    if k_scr is not None:
        k_scr[...] = k
        b_scr[...] = b
        k_row = lambda r: k_scr[r:r + 1, :]
        b_row = lambda r: b_scr[r:r + 1, :]
    else:
        k_row = lambda r: k[r:r + 1, :]
        b_row = lambda r: b[r:r + 1, :]
    sub_row = lax.broadcasted_iota(jnp.int32, (SUB, 1), 0)
    if a_mat is None:
        ob = jnp.zeros((SUB, HD), F32)
        for s in range(SUB):
            a_s = jnp.sum(q * k_row(s) * jnp.exp(b - b_row(s)), axis=-1, keepdims=True)
            ob = ob + jnp.where(sub_row >= s, a_s, 0.0) * v[s:s + 1, :]
        o = o + ob
    else:
        lane = lax.broadcasted_iota(jnp.int32, (SUB, c), 1)
        blocks = []
        for blk in range(c // SUB):
            sl = slice(blk * SUB, (blk + 1) * SUB)
            qb_, bb_ = q[sl], b[sl]
            acc = jnp.zeros((SUB, c), F32)
            for s in range(SUB):
                r = blk * SUB + s
                a_s = jnp.sum(qb_ * k_row(r) * jnp.exp(bb_ - b_row(r)), axis=-1, keepdims=True)
                acc = jnp.where(lane == r, a_s, acc)
            blocks.append(jnp.where(lane <= sub_row + blk * SUB, acc, 0.0))
        a_mat = a_mat + jnp.concatenate(blocks, axis=0)
        o = o + jnp.dot(a_mat.astype(BF16), vb, preferred_element_type=F32)

    b_last = b[c - 1:c, :]
    kt = (k * jnp.exp(b_last - b)).astype(BF16)
    s_new = _row_to_col(jnp.exp(b_last)) * s0 + lax.dot_general(kt, vb, TN_DIMS, preferred_element_type=F32)
    return o, s_new


def _hg_gates(hq, hf, lb):
    sig = _sigmoid(hf)
    lf = jnp.log(lb + (1.0 - lb) * sig)
    k = (1.0 - lb) * _sigmoid(-hf)
    return _silu(hq), k, lf


def _hg_prompt_body(hq_ref, hf_ref, hi_ref, hg_ref, lb_ref, gn_ref, o_ref, s_ref, k_scr, b_scr, *, chunk):
    t = hq_ref.shape[0]
    n_heads = hq_ref.shape[1] // HD
    gn = gn_ref[...]

    def step(ci, states):
        rows = pl.ds(pl.multiple_of(ci * chunk, chunk), chunk)
        new_states = []
        for h in range(n_heads):
            cs = slice(h * HD, (h + 1) * HD)
            q, k, lf = _hg_gates(hq_ref[rows, cs], hf_ref[rows, cs], lb_ref[:, cs])
            o, s = _hg_chunk(q, k, hi_ref[rows, cs], lf, states[h], k_scr.at[h], b_scr.at[h])
            o_ref[rows, cs] = (_rms(o, gn) * _silu(hg_ref[rows, cs])).astype(o_ref.dtype)
            new_states.append(s)
        return tuple(new_states)

    states = lax.fori_loop(0, t // chunk, step, tuple(jnp.zeros((HD, HD), F32) for _ in range(n_heads)))
    for h in range(n_heads):
        s_ref[0, h] = states[h]


def _hg_prompt(z, lb, gn, nb, t, offs, heads_per_step=4):
    hw = heads_per_step * HD
    qb, fb, ib, gb = (offs[n] // hw for n in ("hq", "hf", "hi", "hg"))
    chunk = min(128, t)
    col = lambda base: pl.BlockSpec((t, hw), lambda b, h: (b, base + h))
    return pl.pallas_call(
        functools.partial(_hg_prompt_body, chunk=chunk),
        grid=(nb, HG_H // heads_per_step),
        in_specs=[col(qb), col(fb), col(ib), col(gb),
                  pl.BlockSpec((1, hw), lambda b, h: (0, h)),
                  pl.BlockSpec((1, HD), lambda b, h: (0, 0))],
        out_specs=[pl.BlockSpec((t, hw), lambda b, h: (b, h)),
                   pl.BlockSpec((1, heads_per_step, HD, HD), lambda b, h: (b, h, 0, 0))],
        out_shape=[jax.ShapeDtypeStruct((nb * t, HG_H * HD), BF16),
                   jax.ShapeDtypeStruct((nb, HG_H, HD, HD), F32)],
        scratch_shapes=[pltpu.VMEM((heads_per_step, chunk, HD), F32)] * 2,
        compiler_params=_cparams("parallel", "parallel"),
        name="hgrn2_prompt",
    )(z, z, z, z, lb, gn)


def _hg_sample_body(hq_ref, hf_ref, hi_ref, hg_ref, lb_ref, gn_ref, s0_ref, o_ref, s_ref, *, n_tok):
    valid = lax.broadcasted_iota(jnp.int32, (PAD_L, 1), 0) < n_tok
    gn = gn_ref[...]
    for h in range(HG_H):
        cs = slice(h * HD, (h + 1) * HD)
        q, k, lf = _hg_gates(hq_ref[0, :, cs], hf_ref[0, :, cs], lb_ref[:, cs])
        k = jnp.where(valid, k, 0.0)
        lf = jnp.where(valid, lf, 0.0)
        o, s = _hg_chunk(q, k, hi_ref[0, :, cs], lf, s0_ref[0, 0, h])
        s_ref[0, h] = s
        o_ref[0, :, cs] = (_rms(o, gn) * _silu(hg_ref[0, :, cs])).astype(o_ref.dtype)


def _hg_sample(zs, lb, gn, state, layer, offs, n_tok):
    db = zs.shape[0]
    w = HG_H * HD
    col = lambda name: pl.BlockSpec((1, PAD_L, w), lambda b: (b, 0, offs[name] // w))
    return pl.pallas_call(
        functools.partial(_hg_sample_body, n_tok=n_tok),
        grid=(db,),
        in_specs=[col("hq"), col("hf"), col("hi"), col("hg"),
                  pl.BlockSpec((1, w), lambda b: (0, 0)),
                  pl.BlockSpec((1, HD), lambda b: (0, 0)),
                  pl.BlockSpec((1, 1, HG_H, HD, HD), lambda b: (layer, b, 0, 0, 0))],
        out_specs=[pl.BlockSpec((1, PAD_L, w), lambda b: (b, 0, 0)),
                   pl.BlockSpec((1, HG_H, HD, HD), lambda b: (b, 0, 0, 0))],
        out_shape=[jax.ShapeDtypeStruct((db, PAD_L, w), BF16),
                   jax.ShapeDtypeStruct((db, HG_H, HD, HD), F32)],
        compiler_params=_cparams("parallel"),
        name="hgrn2_sample",
    )(zs, zs, zs, zs, lb, gn, state)


def _lru_gates(xc, wa_ref, ba, wx_ref, bx, lam):
    nblk = xc.shape[1] // HD
    xb = xc.astype(BF16)
    pa = [jnp.dot(xb[:, i * HD:(i + 1) * HD], wa_ref[i], preferred_element_type=F32) for i in range(nblk)]
    px = [jnp.dot(xb[:, i * HD:(i + 1) * HD], wx_ref[i], preferred_element_type=F32) for i in range(nblk)]
    r = _sigmoid(jnp.concatenate(pa, axis=1) + ba)
    ig = _sigmoid(jnp.concatenate(px, axis=1) + bx)
    log_a = -LRU_C * r * _softplus(-lam)
    a = jnp.exp(log_a)
    u = jnp.sqrt(-jnp.tanh(log_a) * (1.0 + a * a)) * (ig * xc)
    return a, u


def _lru_prompt_body(lx_ref, ly_ref, cw_ref, cb_ref, wa_ref, ba_ref, wx_ref, bx_ref, lam_ref,
                     o_ref, h_ref, *, rt):
    t, wt = lx_ref.shape
    cw = cw_ref[...]
    sub_row = lax.broadcasted_iota(jnp.int32, (SUB, 1), 0)
    pos = lax.broadcasted_iota(jnp.int32, (rt, 1), 0) & (SUB - 1)

    def tile(ti, carry):
        h, tail = carry
        rows = pl.ds(pl.multiple_of(ti * rt, rt), rt)
        x = lx_ref[rows, :]
        xc = cb_ref[...] + x * cw[CONV_W - 1:CONV_W, :]
        for d in range(1, CONV_W):
            xr = pltpu.roll(x, d, 0)
            head = jnp.where(sub_row < d, pltpu.roll(tail, d, 0), xr[0:SUB])
            xs = jnp.concatenate([head, xr[SUB:]], axis=0) if rt > SUB else head
            xc = xc + xs * cw[CONV_W - 1 - d:CONV_W - d, :]
        a, u = _lru_gates(xc, wa_ref, ba_ref[...], wx_ref, bx_ref[...], lam_ref[...])
        d = 1
        while d < SUB:
            keep = pos >= d
            u = a * jnp.where(keep, pltpu.roll(u, d, 0), 0.0) + u
            a = a * jnp.where(keep, pltpu.roll(a, d, 0), 1.0)
            d *= 2
        gelu = _gelu_tanh(ly_ref[rows, :])
        for g in range(rt // SUB):
            sl = slice(g * SUB, (g + 1) * SUB)
            hs = a[sl] * h + u[sl]
            h = hs[SUB - 1:SUB, :]
            o_ref[pl.ds(pl.multiple_of(ti * rt, rt) + g * SUB, SUB), :] = (hs * gelu[sl]).astype(o_ref.dtype)
        return h, x[rt - SUB:rt, :]

    h, _ = lax.fori_loop(0, t // rt, tile, (jnp.zeros((1, wt), F32), jnp.zeros((SUB, wt), F32)))
    h_ref[0] = h


def _lru_prompt(z, p, nb, t, offs):
    w = p["lru_conv_b"].shape[1]
    wt = SEG
    nblk = wt // HD
    rt = min(128, t)
    xb, yb = offs["lx"] // wt, offs["ly"] // wt
    vec = pl.BlockSpec((1, wt), lambda b, c: (0, c))
    gate_w = pl.BlockSpec((nblk, HD, HD), lambda b, c: (c, 0, 0))
    return pl.pallas_call(
        functools.partial(_lru_prompt_body, rt=rt),
        grid=(nb, w // wt),
        in_specs=[pl.BlockSpec((t, wt), lambda b, c: (b, xb + c)),
                  pl.BlockSpec((t, wt), lambda b, c: (b, yb + c)),
                  pl.BlockSpec((CONV_W, wt), lambda b, c: (0, c)),
                  vec, gate_w, vec, gate_w, vec, vec],
        out_specs=[pl.BlockSpec((t, wt), lambda b, c: (b, c)),
                   pl.BlockSpec((1, 1, wt), lambda b, c: (b, 0, c))],
        out_shape=[jax.ShapeDtypeStruct((nb * t, w), BF16),
                   jax.ShapeDtypeStruct((nb, 1, w), F32)],
        compiler_params=_cparams("parallel", "parallel"),
        name="lru_prompt",
    )(z, z, p["lru_conv_w"], p["lru_conv_b"], p["lru_w_a"], p["lru_b_a"], p["lru_w_x"], p["lru_b_x"],
      p["lru_lambda"])


def _lru_sample_body(*refs, n_tok):
    lx = refs[0:n_tok]
    ly = refs[n_tok:2 * n_tok]
    buf = refs[2 * n_tok:2 * n_tok + CONV_W - 1]
    h0_ref, cw_ref, cb_ref, wa_ref, ba_ref, wx_ref, bx_ref, lam_ref = refs[2 * n_tok + CONV_W - 1:2 * n_tok + CONV_W + 7]
    outs = refs[2 * n_tok + CONV_W + 7:]
    o_refs, h_ref = outs[0:n_tok], outs[n_tok]
    cw = cw_ref[...]
    xp = [r[0] for r in buf] + [r[...] for r in lx]
    h = h0_ref[0]
    for t in range(n_tok):
        xc = cb_ref[...]
        for j in range(CONV_W):
            xc = xc + xp[t + j] * cw[j:j + 1, :]
        a, u = _lru_gates(xc, wa_ref, ba_ref[...], wx_ref, bx_ref[...], lam_ref[...])
        h = a * h + u
        o_refs[t][...] = (h * _gelu_tanh(ly[t][...])).astype(o_refs[t].dtype)
    h_ref[...] = h


def _lru_sample(zs2, p, state_lru, state_conv, layer, offs, n_tok, nz):
    db = zs2.shape[0]
    w = p["lru_conv_b"].shape[1]
    wt = SEG
    nblk = wt // HD
    sc = state_conv.reshape(state_conv.shape[0], db, (CONV_W - 1) * w)
    tok = lambda name, l: pl.BlockSpec((db, wt), lambda c: (0, (l * nz + offs[name]) // wt + c))
    vec = pl.BlockSpec((1, wt), lambda c: (0, c))
    gate_w = pl.BlockSpec((nblk, HD, HD), lambda c: (c, 0, 0))
    in_specs = ([tok("lx", l) for l in range(n_tok)] + [tok("ly", l) for l in range(n_tok)]
                + [pl.BlockSpec((1, db, wt), lambda c, j=j: (layer, 0, j * (w // wt) + c)) for j in range(CONV_W - 1)]
                + [pl.BlockSpec((1, db, wt), lambda c: (layer, 0, c)),
                   pl.BlockSpec((CONV_W, wt), lambda c: (0, c)),
                   vec, gate_w, vec, gate_w, vec, vec])
    out_spec = pl.BlockSpec((db, wt), lambda c: (0, c))
    outs = pl.pallas_call(
        functools.partial(_lru_sample_body, n_tok=n_tok),
        grid=(w // wt,),
        in_specs=in_specs,
        out_specs=[out_spec] * (n_tok + 1),
        out_shape=[jax.ShapeDtypeStruct((db, w), BF16)] * n_tok + [jax.ShapeDtypeStruct((db, w), F32)],
        compiler_params=_cparams("parallel"),
        name="lru_sample",
    )(*([zs2] * (2 * n_tok) + [sc] * (CONV_W - 1)
        + [state_lru, p["lru_conv_w"], p["lru_conv_b"], p["lru_w_a"], p["lru_b_a"], p["lru_w_x"], p["lru_b_x"],
           p["lru_lambda"]]))
    return jnp.stack(outs[:n_tok], axis=1).reshape(db * n_tok, w), outs[n_tok]


def _xattn_prompt_body(q_ref, k_ref, v_ref, o_ref):
    scale = HD ** -0.5
    for h in range(X_H):
        cs = slice(h * HD, (h + 1) * HD)
        q = (q_ref[:, cs] * scale).astype(BF16)
        s = lax.dot_general(q, k_ref[0, :, cs].astype(BF16), NT_DIMS, preferred_element_type=F32)
        p = jnp.exp(s - jnp.max(s, axis=-1, keepdims=True))
        l = jnp.sum(p, axis=-1, keepdims=True)
        o = jnp.dot(p.astype(BF16), v_ref[0, :, cs].astype(BF16), preferred_element_type=F32) / l
        o_ref[:, cs] = o.astype(o_ref.dtype)


def _xattn_prompt(q, mk, mv, nb, t):
    w = X_H * HD
    m = mk.shape[1]
    tq = min(512, t)
    return pl.pallas_call(
        _xattn_prompt_body,
        grid=(nb, t // tq),
        in_specs=[pl.BlockSpec((tq, w), lambda b, i: (b * (t // tq) + i, 0)),
                  pl.BlockSpec((1, m, w), lambda b, i: (b, 0, 0)),
                  pl.BlockSpec((1, m, w), lambda b, i: (b, 0, 0))],
        out_specs=pl.BlockSpec((tq, w), lambda b, i: (b * (t // tq) + i, 0)),
        out_shape=jax.ShapeDtypeStruct((nb * t, w), BF16),
        compiler_params=_cparams("parallel", "arbitrary"),
        name="xattn_prompt",
    )(q, mk, mv)


def _xattn_sample_body(q_ref, k_ref, v_ref, o_ref, *, n_seq):
    scale = HD ** -0.5
    for i in range(n_seq):
        q = (q_ref[i] * scale).astype(BF16)
        s = lax.dot_general(q, k_ref[0, i].astype(BF16), NT_DIMS, preferred_element_type=F32)
        r = lax.broadcasted_iota(jnp.int32, s.shape, 0)
        c = lax.broadcasted_iota(jnp.int32, s.shape, 1)
        s = jnp.where((r & (X_H - 1)) == (c & (X_H - 1)), s, NEG_INF)
        p = jnp.exp(s - jnp.max(s, axis=-1, keepdims=True))
        l = jnp.sum(p, axis=-1, keepdims=True)
        o = jnp.dot(p.astype(BF16), v_ref[0, i].astype(BF16), preferred_element_type=F32) / l
        o_ref[i] = o.astype(o_ref.dtype)


def _xattn_sample(q, cache_mk, cache_mv, layer):
    db, nrow, _ = q.shape
    mx = cache_mk.shape[2]
    n_seq = min(8, db)
    kv = pl.BlockSpec((1, n_seq, mx, HD), lambda b: (layer, b, 0, 0))
    return pl.pallas_call(
        functools.partial(_xattn_sample_body, n_seq=n_seq),
        grid=(db // n_seq,),
        in_specs=[pl.BlockSpec((n_seq, nrow, HD), lambda b: (b, 0, 0)), kv, kv],
        out_specs=pl.BlockSpec((n_seq, nrow, HD), lambda b: (b, 0, 0)),
        out_shape=jax.ShapeDtypeStruct((db, nrow, HD), BF16),
        compiler_params=_cparams("parallel"),
        name="xattn_sample",
    )(q, cache_mk, cache_mv)


def _segment_offsets(d):
    fox_w, hg_w, lru_w = FOX_H * HD, HG_H * HD, HG_H * HD
    names = [("fq", fox_w), ("fk", fox_w), ("fv", fox_w), ("hq", hg_w), ("hf", hg_w), ("hi", hg_w), ("hg", hg_w),
             ("lx", lru_w), ("ly", lru_w), ("ga", d), ("gb", d), ("gc", d), ("ff", FF_SEG)]
    offs, o = {}, 0
    for n, wdt in names:
        offs[n] = o
        o += wdt
    return offs, o


def _rearrange_w_in(w_in, d):
    fw = FOX_H * HD
    a, ff, b = w_in[..., :3 * fw], w_in[..., 3 * fw:3 * fw + FOX_H], w_in[..., 3 * fw + FOX_H:]
    pad = jnp.zeros(w_in.shape[:-1] + (FF_SEG - FOX_H,), BF16)
    return jnp.concatenate([a.astype(BF16), b.astype(BF16), ff.astype(BF16), pad], axis=-1)


def _pad_tokens(x, db, n_tok):
    return jnp.pad(x.reshape(db, n_tok, x.shape[1]), ((0, 0), (0, PAD_L - n_tok), (0, 0)))


def kernel(x_prompt, x_sample, mem_prompt, cache_k, cache_v, cache_logf, state_hgrn, state_lru, state_conv, cache_mem_k, cache_mem_v, page_table, ffn1_norm, ffn1_w_gate, ffn1_w_up, ffn1_w_down, mix_norm, w_in, fox_b_f, fox_q_norm, fox_k_norm, hg_lb_param, hg_out_norm, lru_conv_w, lru_conv_b, lru_w_a, lru_b_a, lru_w_x, lru_b_x, lru_lambda, w_br_fox, w_br_hg, w_br_lru, w_out, xa_norm, xa_mem_norm, xa_w_q, xa_w_kv, xa_q_norm, xa_k_norm, xa_w_o, ffn2_norm, ffn2_w_gate, ffn2_w_up, ffn2_w_down):
    nb, t, d = x_prompt.shape
    db, n_tok, _ = x_sample.shape
    depth = w_in.shape[0]
    n_mem = mem_prompt.shape[1]
    np_rows = nb * t
    fw, xw = FOX_H * HD, X_H * HD
    lw = lru_conv_b.shape[1]
    assert n_tok <= PAD_L and n_tok >= CONV_W - 1 and t >= CONV_W - 1
    offs, nz = _segment_offsets(d)

    sm = jax.nn.softmax(hg_lb_param.astype(F32), axis=0)
    hg_lb = jnp.cumsum(sm, axis=0) - sm[0]

    x = jnp.concatenate([x_prompt.reshape(np_rows, d), x_sample.reshape(db * n_tok, d)], axis=0)
    mem = mem_prompt.reshape(nb * n_mem, d)
    cmk = cache_mem_k.reshape(depth, db, n_mem * X_H, HD)
    cmv = cache_mem_v.reshape(depth, db, n_mem * X_H, HD)

    cache_sums = _fox_cache_sums(cache_logf)
    w_in_r = _rearrange_w_in(w_in, d)
    w16 = {name: w.astype(BF16) for name, w in dict(
        ffn1_g=ffn1_w_gate, ffn1_u=ffn1_w_up, ffn1_d=ffn1_w_down, ffn2_g=ffn2_w_gate, ffn2_u=ffn2_w_up,
        ffn2_d=ffn2_w_down, br_fox=w_br_fox, br_hg=w_br_hg, br_lru=w_br_lru, out=w_out,
        xa_q=xa_w_q, xa_kv=xa_w_kv, xa_o=xa_w_o).items()}

    outs = {n: [] for n in ("pk", "pv", "plf", "ps", "ph", "pc", "pmk", "pmv", "sk", "sv", "slf", "ss", "sh", "sc")}
    for l in range(depth):
        bf = lambda a: a[l].astype(BF16)
        row = lambda a: a[l].reshape(1, -1).astype(F32)
        lru_p = dict(lru_conv_w=lru_conv_w[l], lru_conv_b=row(lru_conv_b), lru_w_a=bf(lru_w_a), lru_b_a=row(lru_b_a),
                     lru_w_x=bf(lru_w_x), lru_b_x=row(lru_b_x), lru_lambda=row(lru_lambda))
        fox_bias = jnp.pad(row(fox_b_f), ((0, 0), (0, LANES - FOX_H)))
        head_gain = jnp.concatenate([jnp.tile(row(fox_q_norm), (1, FOX_H)), jnp.tile(row(fox_k_norm), (1, FOX_H)),
                                     jnp.ones((1, nz - 2 * fw), F32)], axis=1)

        x = _ffn(x, ffn1_norm[l], w16["ffn1_g"], w16["ffn1_u"], w16["ffn1_d"], l)
        z = _norm_matmul(x, mix_norm[l], w_in_r, l, head_gain, n_norm_cols=2 * fw)
        zs = _pad_tokens(z[np_rows:], db, n_tok)

        plf, c_row = _fox_prep(z, fox_bias, nb, t, offs["ff"])
        of_p = _fox_prompt(z, c_row, nb, t, offs)
        slf, cum = _fox_gates(zs, fox_bias, offs["ff"], n_tok)
        slf = slf.reshape(db, PAD_L, LANES)
        cum = cum.reshape(db, PAD_L, LANES)[:, :, :FOX_H]
        heads = lambda name, rows: zs[:, :rows, offs[name]:offs[name] + fw].reshape(db, rows * FOX_H, HD)
        of_s = _fox_sample(heads("fq", n_tok), heads("fk", PAD_L), heads("fv", PAD_L),
                           cum[:, :n_tok].reshape(db, n_tok * FOX_H, 1), cum.reshape(db, 1, PAD_L * FOX_H),
                           cache_k, cache_v, cache_sums, page_table, l).reshape(db * n_tok, fw)
        oh_p, ps = _hg_prompt(z, row(hg_lb), row(hg_out_norm), nb, t, offs)
        oh_s, ss = _hg_sample(zs, row(hg_lb), row(hg_out_norm), state_hgrn, l, offs, n_tok)
        ol_p, ph = _lru_prompt(z, lru_p, nb, t, offs)
        ol_s, sh = _lru_sample(zs.reshape(db, PAD_L * nz), lru_p, state_lru, state_conv, l, offs, n_tok, nz)

        unpad = lambda a: a[:, :n_tok].reshape(db * n_tok, a.shape[2])
        o_fox = jnp.concatenate([of_p, of_s], axis=0)
        o_hg = jnp.concatenate([oh_p, unpad(oh_s)], axis=0)
        o_lru = jnp.concatenate([ol_p, ol_s], axis=0)
        x = _merge(o_fox, o_hg, o_lru, w16["br_fox"], w16["br_hg"], w16["br_lru"], z, offs["ga"], w16["out"], x, l)

        xq = _norm_matmul(x, xa_norm[l], w16["xa_q"], l, jnp.tile(row(xa_q_norm), (1, X_H)), n_norm_cols=xw)
        kv_gain = jnp.concatenate([jnp.tile(row(xa_k_norm), (1, X_H)), jnp.ones((1, xw), F32)], axis=1)
        mkv = _norm_matmul(mem, xa_mem_norm[l], w16["xa_kv"], l, kv_gain, n_norm_cols=xw)
        mk, mv = mkv[:, :xw].reshape(nb, n_mem, xw), mkv[:, xw:].reshape(nb, n_mem, xw)
        ox_p = _xattn_prompt(xq[:np_rows], mk, mv, nb, t)
        ox_s = _xattn_sample(xq[np_rows:].reshape(db, n_tok * X_H, HD), cmk, cmv, l).reshape(db * n_tok, xw)
        x = _matmul_residual(jnp.concatenate([ox_p, ox_s], axis=0), w16["xa_o"], x, l)

        x = _ffn(x, ffn2_norm[l], w16["ffn2_g"], w16["ffn2_u"], w16["ffn2_d"], l)

        zp, zsm = z[:np_rows], z[np_rows:]
        outs["pk"].append(zp[:, offs["fk"]:offs["fk"] + fw])
        outs["pv"].append(zp[:, offs["fv"]:offs["fv"] + fw])
        outs["plf"].append(plf)
        outs["ps"].append(ps)
        outs["ph"].append(ph.reshape(nb, lw))
        outs["pc"].append(zp[:, offs["lx"]:offs["lx"] + lw].reshape(nb, t, lw)[:, t - (CONV_W - 1):])
        outs["pmk"].append(mk.reshape(nb, n_mem, X_H, HD))
        outs["pmv"].append(mv.reshape(nb, n_mem, X_H, HD))
        outs["sk"].append(zsm[:, offs["fk"]:offs["fk"] + fw].reshape(db, n_tok, FOX_H, HD))
        outs["sv"].append(zsm[:, offs["fv"]:offs["fv"] + fw].reshape(db, n_tok, FOX_H, HD))
        outs["slf"].append(slf[:, :n_tok, :FOX_H])
        outs["ss"].append(ss)
        outs["sh"].append(sh)
        outs["sc"].append(zsm[:, offs["lx"]:offs["lx"] + lw].reshape(db, n_tok, lw)[:, n_tok - (CONV_W - 1):])

    st = lambda n: jnp.stack(outs[n])
    return (x[:np_rows].reshape(nb, t, d), x[np_rows:].reshape(db, n_tok, d),
            st("pk").reshape(depth, nb, t, FOX_H, HD), st("pv").reshape(depth, nb, t, FOX_H, HD),
            st("plf"), st("ps"), st("ph"), st("pc"), st("pmk"), st("pmv"),
            st("sk"), st("sv"), st("slf"), st("ss"), st("sh"), st("sc"))
```

```python
import functools

import jax
import jax.numpy as jnp
from jax import lax
from jax.experimental import pallas as pl
from jax.experimental.pallas import tpu as pltpu

F32 = jnp.float32
BF16 = jnp.bfloat16
EPS = 1e-6
NEG_INF = -1e30
HD = 128
FOX_H = 8
HG_H = 8
X_H = 4
LRU_C = 8.0
CONV_W = 4
SUB = 8
LANES = 128
SEG = 512
FF_SEG = 1024
VMEM_LIMIT = 56 * 1024 * 1024
ROW_TILES = (768, 512, 256, 128, 64, 32, 16, 8)
PAD_L = 8
PAGE_GROUP = 16

NT_DIMS = (((1,), (1,)), ((), ()))
TN_DIMS = (((0,), (0,)), ((), ()))


def _cparams(*sem):
    return pltpu.CompilerParams(dimension_semantics=sem, vmem_limit_bytes=VMEM_LIMIT)


def _row_tile(n, cap=ROW_TILES[0]):
    return next(t for t in ROW_TILES if t <= cap and n % t == 0)


def _col_tile(n, cands, also_divides=0):
    return next(t for t in cands if n % t == 0 and also_divides % t == 0)


def _rms(x, g):
    return x * lax.rsqrt(jnp.mean(x * x, axis=-1, keepdims=True) + EPS) * g


def _sigmoid(x):
    return jax.nn.sigmoid(x)


def _silu(x):
    return x * jax.nn.sigmoid(x)


def _log_sigmoid(x):
    return jnp.minimum(x, 0.0) - jnp.log1p(jnp.exp(-jnp.abs(x)))


def _softplus(x):
    return jnp.maximum(x, 0.0) + jnp.log1p(jnp.exp(-jnp.abs(x)))


def _gelu_tanh(x):
    return 0.5 * x * (1.0 + jnp.tanh(0.7978845608028654 * (x + 0.044715 * (x * x * x))))


def _cumsum_lanes(x, reverse=False):
    lane = lax.broadcasted_iota(jnp.int32, x.shape, 1)
    d = 1
    while d < LANES:
        if reverse:
            x = x + jnp.where(lane < LANES - d, pltpu.roll(x, LANES - d, 1), 0.0)
        else:
            x = x + jnp.where(lane >= d, pltpu.roll(x, d, 1), 0.0)
        d *= 2
    return x


def _cumsum_rows(x):
    c = x.shape[0]
    row = lax.broadcasted_iota(jnp.int32, (c, 1), 0)
    d = 1
    while d < c:
        x = x + jnp.where(row >= d, pltpu.roll(x, d, 0), 0.0)
        d *= 2
    return x


def _row_to_col(r):
    return jnp.broadcast_to(r, (SUB, r.shape[1])).T[:, 0:1]


def _ffn_body(x_ref, g_ref, wg_ref, wu_ref, wd_ref, o_ref, h_ref, acc_ref):
    j = pl.program_id(1)

    @pl.when(j == 0)
    def _():
        h_ref[...] = _rms(x_ref[...], g_ref[...]).astype(BF16)
        acc_ref[...] = jnp.zeros_like(acc_ref)

    h = h_ref[...]
    gt = jnp.dot(h, wg_ref[...], preferred_element_type=F32)
    ut = jnp.dot(h, wu_ref[...], preferred_element_type=F32)
    a = (_silu(gt) * ut).astype(BF16)
    acc_ref[...] += jnp.dot(a, wd_ref[...], preferred_element_type=F32)

    @pl.when(j == pl.num_programs(1) - 1)
    def _():
        o_ref[...] = x_ref[...] + 0.5 * acc_ref[...]


def _ffn(x, g, wg, wu, wd, layer, tf=512):
    n, d = x.shape
    f = wg.shape[2]
    tm = _row_tile(n)
    tf = min(tf, f)
    return pl.pallas_call(
        _ffn_body,
        grid=(n // tm, f // tf),
        in_specs=[
            pl.BlockSpec((tm, d), lambda i, j: (i, 0)),
            pl.BlockSpec((1, d), lambda i, j: (0, 0)),
            pl.BlockSpec((None, d, tf), lambda i, j: (layer, 0, j)),
            pl.BlockSpec((None, d, tf), lambda i, j: (layer, 0, j)),
            pl.BlockSpec((None, tf, d), lambda i, j: (layer, j, 0)),
        ],
        out_specs=pl.BlockSpec((tm, d), lambda i, j: (i, 0)),
        out_shape=jax.ShapeDtypeStruct((n, d), F32),
        scratch_shapes=[pltpu.VMEM((tm, d), BF16), pltpu.VMEM((tm, d), F32)],
        compiler_params=_cparams("parallel", "arbitrary"),
        name="ffn",
    )(x, g.reshape(1, d), wg, wu, wd)


def _nmm_body(x_ref, g_ref, w_ref, gn_ref, o_ref, h_ref, *, n_norm_tiles):
    j = pl.program_id(1)

    @pl.when(j == 0)
    def _():
        h_ref[...] = _rms(x_ref[...], g_ref[...]).astype(BF16)

    y = jnp.dot(h_ref[...], w_ref[...], preferred_element_type=F32)
    if n_norm_tiles == 0:
        o_ref[...] = y
    else:
        @pl.when(j < n_norm_tiles)
        def _():
            tn = y.shape[1]
            parts = [_rms(y[:, c * HD:(c + 1) * HD], gn_ref[:, c * HD:(c + 1) * HD]) for c in range(tn // HD)]
            o_ref[...] = jnp.concatenate(parts, axis=1)

        @pl.when(j >= n_norm_tiles)
        def _():
            o_ref[...] = y


def _norm_matmul(x, g, w, layer, head_gain=None, n_norm_cols=0):
    n, d = x.shape
    nout = w.shape[2]
    tm = _row_tile(n)
    tn = _col_tile(nout, (2048, 1024, 512, 256, 128), n_norm_cols)
    if head_gain is None:
        head_gain = jnp.ones((1, nout), F32)
    return pl.pallas_call(
        functools.partial(_nmm_body, n_norm_tiles=n_norm_cols // tn),
        grid=(n // tm, nout // tn),
        in_specs=[
            pl.BlockSpec((tm, d), lambda i, j: (i, 0)),
            pl.BlockSpec((1, d), lambda i, j: (0, 0)),
            pl.BlockSpec((None, d, tn), lambda i, j: (layer, 0, j)),
            pl.BlockSpec((1, tn), lambda i, j: (0, j)),
        ],
        out_specs=pl.BlockSpec((tm, tn), lambda i, j: (i, j)),
        out_shape=jax.ShapeDtypeStruct((n, nout), F32),
        scratch_shapes=[pltpu.VMEM((tm, d), BF16)],
        compiler_params=_cparams("parallel", "arbitrary"),
        name="norm_matmul",
    )(x, g.reshape(1, d), w, head_gain)


def _mmres_body(a_ref, w_ref, r_ref, o_ref):
    o_ref[...] = r_ref[...] + jnp.dot(a_ref[...], w_ref[...], preferred_element_type=F32)


def _matmul_residual(a, w, res, layer):
    n, k = a.shape
    nout = w.shape[2]
    tm = _row_tile(n)
    tn = _col_tile(nout, (2048, 1024, 512, 256, 128))
    return pl.pallas_call(
        _mmres_body,
        grid=(n // tm, nout // tn),
        in_specs=[
            pl.BlockSpec((tm, k), lambda i, j: (i, 0)),
            pl.BlockSpec((None, k, tn), lambda i, j: (layer, 0, j)),
            pl.BlockSpec((tm, tn), lambda i, j: (i, j)),
        ],
        out_specs=pl.BlockSpec((tm, tn), lambda i, j: (i, j)),
        out_shape=jax.ShapeDtypeStruct((n, nout), F32),
        compiler_params=_cparams("parallel", "arbitrary"),
        name="matmul_residual",
    )(a, w, res)


def _merge_body(of_ref, oh_ref, ol_ref, wf_ref, wh_ref, wl_ref, ga_ref, gb_ref, gc_ref, wo_ref, x_ref, o_ref):
    j = pl.program_id(1)
    y = _sigmoid(ga_ref[...]) * jnp.dot(of_ref[...], wf_ref[...], preferred_element_type=F32)
    y = y + _sigmoid(gb_ref[...]) * jnp.dot(oh_ref[...], wh_ref[...], preferred_element_type=F32)
    y = y + _sigmoid(gc_ref[...]) * jnp.dot(ol_ref[...], wl_ref[...], preferred_element_type=F32)
    part = jnp.dot(y.astype(BF16), wo_ref[...], preferred_element_type=F32)

    @pl.when(j == 0)
    def _():
        o_ref[...] = x_ref[...] + part

    @pl.when(j > 0)
    def _():
        o_ref[...] += part


def _merge(o_fox, o_hg, o_lru, wf, wh, wl, z, gate_off, w_out, x, layer, tn=SEG):
    n, k = o_fox.shape
    d = wf.shape[2]
    tm = _row_tile(n, cap=512)
    tn = min(tn, d)
    gblk = gate_off // tn
    dblk = d // tn
    act = pl.BlockSpec((tm, k), lambda i, j: (i, 0))
    wsp = pl.BlockSpec((None, k, tn), lambda i, j: (layer, 0, j))
    full = pl.BlockSpec((tm, d), lambda i, j: (i, 0))
    return pl.pallas_call(
        _merge_body,
        grid=(n // tm, dblk),
        in_specs=[act, act, act, wsp, wsp, wsp,
                  pl.BlockSpec((tm, tn), lambda i, j: (i, gblk + j)),
                  pl.BlockSpec((tm, tn), lambda i, j: (i, gblk + dblk + j)),
                  pl.BlockSpec((tm, tn), lambda i, j: (i, gblk + 2 * dblk + j)),
                  pl.BlockSpec((None, tn, d), lambda i, j: (layer, j, 0)),
                  full],
        out_specs=full,
        out_shape=jax.ShapeDtypeStruct((n, d), F32),
        compiler_params=_cparams("parallel", "arbitrary"),
        name="merge",
    )(o_fox, o_hg, o_lru, wf, wh, wl, z, z, z, w_out, x)


def _fox_prep_body(zf_ref, bias_ref, lf_ref, c_ref):
    t = zf_ref.shape[0]
    lf = _log_sigmoid(zf_ref[:, 0:LANES] + bias_ref[...])
    lf_ref[0] = lf[:, 0:FOX_H]
    x = lf.T[0:FOX_H, :]
    carry = jnp.zeros((FOX_H, 1), F32)
    for blk in range(t // LANES):
        cs = _cumsum_lanes(x[:, blk * LANES:(blk + 1) * LANES]) + carry
        c_ref[0, :, blk * LANES:(blk + 1) * LANES] = cs
        carry = cs[:, LANES - 1:LANES]


def _fox_prep(z, bias, nb, t, ff_off):
    return pl.pallas_call(
        _fox_prep_body,
        grid=(nb,),
        in_specs=[pl.BlockSpec((t, SEG), lambda b: (b, ff_off // SEG)),
                  pl.BlockSpec((1, LANES), lambda b: (0, 0))],
        out_specs=[pl.BlockSpec((1, t, FOX_H), lambda b: (b, 0, 0)),
                   pl.BlockSpec((1, FOX_H, t), lambda b: (b, 0, 0))],
        out_shape=[jax.ShapeDtypeStruct((nb, t, FOX_H), F32),
                   jax.ShapeDtypeStruct((nb, FOX_H, t), F32)],
        compiler_params=_cparams("parallel"),
        name="fox_prep",
    )(z, bias)


def _fox_prompt_body(q_ref, k_ref, v_ref, c_ref, o_ref, *, tq):
    t = q_ref.shape[0]
    scale = HD ** -0.5
    c = c_ref[0, 0]
    ccol = _row_to_col(c)
    k = k_ref[...].astype(BF16)
    v = v_ref[...].astype(BF16)
    row = lax.broadcasted_iota(jnp.int32, (tq, tq), 0)
    col = lax.broadcasted_iota(jnp.int32, (tq, tq), 1)
    for i in range(t // tq):
        lo, hi = i * tq, (i + 1) * tq
        q = (q_ref[lo:hi, :] * scale).astype(BF16)
        cq = ccol[lo:hi]
        sd = lax.dot_general(q, k[lo:hi], NT_DIMS, preferred_element_type=F32) + cq - c[:, lo:hi]
        sd = jnp.where(col <= row, sd, NEG_INF)
        m = jnp.max(sd, axis=-1, keepdims=True)
        if i > 0:
            sp = lax.dot_general(q, k[0:lo], NT_DIMS, preferred_element_type=F32) + cq - c[:, 0:lo]
            m = jnp.maximum(m, jnp.max(sp, axis=-1, keepdims=True))
        pd = jnp.exp(sd - m)
        l = jnp.sum(pd, axis=-1, keepdims=True)
        o = jnp.dot(pd.astype(BF16), v[lo:hi], preferred_element_type=F32)
        if i > 0:
            pp = jnp.exp(sp - m)
            l = l + jnp.sum(pp, axis=-1, keepdims=True)
            o = o + jnp.dot(pp.astype(BF16), v[0:lo], preferred_element_type=F32)
        o_ref[lo:hi, :] = (o / l).astype(o_ref.dtype)


def _fox_prompt(z, c_row, nb, t, offs):
    qb, kb, vb = (offs[n] // HD for n in ("fq", "fk", "fv"))
    tq = min(256, t)
    return pl.pallas_call(
        functools.partial(_fox_prompt_body, tq=tq),
        grid=(nb, FOX_H),
        in_specs=[pl.BlockSpec((t, HD), lambda b, h: (b, qb + h)),
                  pl.BlockSpec((t, HD), lambda b, h: (b, kb + h)),
                  pl.BlockSpec((t, HD), lambda b, h: (b, vb + h)),
                  pl.BlockSpec((1, 1, 1, t), lambda b, h: (b, h, 0, 0))],
        out_specs=pl.BlockSpec((t, HD), lambda b, h: (b, h)),
        out_shape=jax.ShapeDtypeStruct((nb * t, FOX_H * HD), BF16),
        compiler_params=_cparams("parallel", "parallel"),
        name="fox_prompt",
    )(z, z, z, c_row.reshape(nb, FOX_H, 1, t))


def _strided_roll_add(x, shift, mask=None):
    y = pltpu.roll(x, shift, 1)
    return x + (y if mask is None else jnp.where(mask, y, 0.0))


def _fox_cache_sums_body(lf_ref, o_ref):
    lf = lf_ref[0]
    n = lf.shape[1]
    lane = lax.broadcasted_iota(jnp.int32, lf.shape, 1)
    incl, tot = lf, lf
    d = FOX_H
    while d < n:
        incl = _strided_roll_add(incl, n - d, lane < n - d)
        tot = _strided_roll_add(tot, d)
        d *= 2
    o_ref[0, :, 0:n] = incl - lf
    o_ref[0, :, n:2 * n] = tot


def _fox_cache_sums(cache_logf):
    depth, n_pool, page, nh = cache_logf.shape
    n = page * nh
    blk = next(t for t in (256, 128, 64, 32, 16, 8, n_pool) if n_pool % t == 0)
    sums = pl.pallas_call(
        _fox_cache_sums_body,
        grid=(depth, n_pool // blk),
        in_specs=[pl.BlockSpec((1, blk, n), lambda l, i: (l, i, 0))],
        out_specs=pl.BlockSpec((1, blk, 2 * n), lambda l, i: (l, i, 0)),
        out_shape=jax.ShapeDtypeStruct((depth, n_pool, 2 * n), F32),
        compiler_params=_cparams("parallel", "parallel"),
        name="fox_cache_sums",
    )(cache_logf.reshape(depth, n_pool, n))
    return sums.reshape(depth, n_pool, 2, n)


def _fox_gates_body(zf_ref, bias_ref, lf_ref, cum_ref, *, n_tok):
    rows = zf_ref.shape[0] * PAD_L
    lf = _log_sigmoid(zf_ref[:, :, 0:LANES].reshape(rows, LANES) + bias_ref[...])
    lf_ref[...] = lf
    pos = lax.broadcasted_iota(jnp.int32, (rows, 1), 0) & (PAD_L - 1)
    cum = jnp.where(pos < n_tok, lf, 0.0)
    d = 1
    while d < PAD_L:
        cum = cum + jnp.where(pos >= d, pltpu.roll(cum, d, 0), 0.0)
        d *= 2
    cum_ref[...] = cum


def _fox_gates(zs, bias, ff_off, n_tok):
    db = zs.shape[0]
    out = jax.ShapeDtypeStruct((db * PAD_L, LANES), F32)
    return pl.pallas_call(
        functools.partial(_fox_gates_body, n_tok=n_tok),
        grid=(1,),
        in_specs=[pl.BlockSpec((db, PAD_L, SEG), lambda i: (0, 0, ff_off // SEG)),
                  pl.BlockSpec((1, LANES), lambda i: (0, 0))],
        out_specs=[pl.BlockSpec((db * PAD_L, LANES), lambda i: (0, 0))] * 2,
        out_shape=[out, out],
        compiler_params=_cparams("arbitrary"),
        name="fox_gates",
    )(zs, bias)


def _fox_sample_body(pt_ref, q_ref, kn_ref, vn_ref, ccol_ref, crow_ref, *rest, n_grp):
    kp, vp, sums = rest[0:n_grp], rest[n_grp:2 * n_grp], rest[2 * n_grp:3 * n_grp]
    o_ref, m_ref, l_ref, acc_ref, carry_ref = rest[3 * n_grp:]
    p = pl.program_id(1)
    q = (q_ref[0] * HD ** -0.5).astype(BF16)

    def same_head(shape):
        r = lax.broadcasted_iota(jnp.int32, shape, 0)
        c = lax.broadcasted_iota(jnp.int32, shape, 1)
        return (r & (FOX_H - 1)) == (c & (FOX_H - 1)), r, c

    @pl.when(p == 0)
    def _():
        s = lax.dot_general(q, kn_ref[0].astype(BF16), NT_DIMS, preferred_element_type=F32)
        s = s + ccol_ref[0] - crow_ref[0]
        ok, r, c = same_head(s.shape)
        s = jnp.where(ok & ((c >> 3) <= (r >> 3)), s, NEG_INF)
        m = jnp.max(s, axis=-1, keepdims=True)
        pn = jnp.exp(s - m)
        m_ref[...] = m
        l_ref[...] = jnp.sum(pn, axis=-1, keepdims=True)
        acc_ref[...] = jnp.dot(pn.astype(BF16), vn_ref[0].astype(BF16), preferred_element_type=F32)
        carry_ref[...] = jnp.zeros_like(carry_ref)

    ccol = ccol_ref[0]
    carry = carry_ref[...]
    scores = [None] * n_grp
    for j in reversed(range(n_grp)):
        k2 = kp[j][0, 0].reshape(-1, HD).astype(BF16)
        s = lax.dot_general(q, k2, NT_DIMS, preferred_element_type=F32)
        s = s + ccol + (sums[j][0, 0, 0:1, :] + carry)
        carry = carry + sums[j][0, 0, 1:2, :]
        scores[j] = jnp.where(same_head(s.shape)[0], s, NEG_INF)
    carry_ref[...] = carry
    m_old = m_ref[...]
    m_new = m_old
    for s in scores:
        m_new = jnp.maximum(m_new, jnp.max(s, axis=-1, keepdims=True))
    alpha = jnp.exp(m_old - m_new)
    l_run = alpha * l_ref[...]
    acc = alpha * acc_ref[...]
    for j in range(n_grp):
        pp = jnp.exp(scores[j] - m_new)
        l_run = l_run + jnp.sum(pp, axis=-1, keepdims=True)
        v2 = vp[j][0, 0].reshape(-1, HD).astype(BF16)
        acc = acc + jnp.dot(pp.astype(BF16), v2, preferred_element_type=F32)
    m_ref[...], l_ref[...], acc_ref[...] = m_new, l_run, acc

    @pl.when(p == pl.num_programs(1) - 1)
    def _():
        o_ref[0] = (acc / l_run).astype(o_ref.dtype)


def _fox_sample(q, kn, vn, cum_col, cum_row, cache_k, cache_v, cache_sums, page_table, layer):
    db, nrow, _ = q.shape
    n_pages = page_table.shape[1]
    page = cache_k.shape[2]
    n_grp = next(g for g in (PAGE_GROUP, 4, 2, 1) if n_pages % g == 0)
    n_flat = page * FOX_H

    def page_map(j):
        return lambda b, p, pt: (layer, pt[b * n_pages + n_pages - (p + 1) * n_grp + j], 0, 0, 0)

    def sums_map(j):
        return lambda b, p, pt: (layer, pt[b * n_pages + n_pages - (p + 1) * n_grp + j], 0, 0)

    seq = lambda r, c: pl.BlockSpec((1, r, c), lambda b, p, pt: (b, 0, 0))
    kv_specs = [pl.BlockSpec((1, 1, page, FOX_H, HD), page_map(j)) for j in range(n_grp)]
    grid_spec = pltpu.PrefetchScalarGridSpec(
        num_scalar_prefetch=1,
        grid=(db, n_pages // n_grp),
        in_specs=[seq(nrow, HD), seq(PAD_L * FOX_H, HD), seq(PAD_L * FOX_H, HD), seq(nrow, 1), seq(1, PAD_L * FOX_H)]
                 + kv_specs + kv_specs
                 + [pl.BlockSpec((1, 1, 2, n_flat), sums_map(j)) for j in range(n_grp)],
        out_specs=seq(nrow, HD),
        scratch_shapes=[pltpu.VMEM((nrow, 1), F32), pltpu.VMEM((nrow, 1), F32), pltpu.VMEM((nrow, HD), F32),
                        pltpu.VMEM((1, n_flat), F32)],
    )
    return pl.pallas_call(
        functools.partial(_fox_sample_body, n_grp=n_grp),
        grid_spec=grid_spec,
        out_shape=jax.ShapeDtypeStruct((db, nrow, HD), BF16),
        compiler_params=_cparams("parallel", "arbitrary"),
        name="fox_sample",
    )(page_table.reshape(-1), q, kn, vn, cum_col, cum_row, *([cache_k] * n_grp), *([cache_v] * n_grp),
      *([cache_sums] * n_grp))


def _hg_chunk(q, k, v, lf, s0, k_scr=None, b_scr=None):
    c = q.shape[0]
    if c > SUB:
        tri = (lax.broadcasted_iota(jnp.int32, (c, c), 1) <= lax.broadcasted_iota(jnp.int32, (c, c), 0)).astype(F32)
        b = jnp.dot(tri, lf, preferred_element_type=F32, precision=lax.Precision.HIGHEST)
    else:
        b = _cumsum_rows(lf)
    o = jnp.dot((q * jnp.exp(b)).astype(BF16), s0.astype(BF16), preferred_element_type=F32)
    vb = v.astype(BF16)

    row = lax.broadcasted_iota(jnp.int32, (c, 1), 0)
    a_mat = None
    half = SUB
    while 2 * half <= c:
        nblk = c // (2 * half)
        shift = (2 * half).bit_length() - 1
        split = jnp.broadcast_to(b.reshape(nblk, 2 * half, HD)[:, half - 1:half, :],
                                 (nblk, 2 * half, HD)).reshape(c, HD)
        right = (row & (2 * half - 1)) >= half
        qt = jnp.where(right, q * jnp.exp(b - split), 0.0).astype(BF16)
        kt = jnp.where(right, 0.0, k * jnp.exp(split - b)).astype(BF16)
        a_l = lax.dot_general(qt, kt, NT_DIMS, preferred_element_type=F32)
        ri = lax.broadcasted_iota(jnp.int32, (c, c), 0) >> shift
        ci = lax.broadcasted_iota(jnp.int32, (c, c), 1) >> shift
        a_l = jnp.where(ri == ci, a_l, 0.0)
        a_mat = a_l if a_mat is None else a_mat + a_l
        half *= 2

    if k_scr is not None:
        k_scr[...] = k
        b_scr[...] = b
        k_row = lambda r: k_scr[r:r + 1, :]
        b_row = lambda r: b_scr[r:r + 1, :]
    else:
        k_row = lambda r: k[r:r + 1, :]
        b_row = lambda r: b[r:r + 1, :]
    sub_row = lax.broadcasted_iota(jnp.int32, (SUB, 1), 0)
    if a_mat is None:
        ob = jnp.zeros((SUB, HD), F32)
        for s in range(SUB):
            a_s = jnp.sum(q * k_row(s) * jnp.exp(b - b_row(s)), axis=-1, keepdims=True)
            ob = ob + jnp.where(sub_row >= s, a_s, 0.0) * v[s:s + 1, :]
        o = o + ob
    else:
        lane = lax.broadcasted_iota(jnp.int32, (SUB, c), 1)
        blocks = []
        for blk in range(c // SUB):
            sl = slice(blk * SUB, (blk + 1) * SUB)
            qb_, bb_ = q[sl], b[sl]
            acc = jnp.zeros((SUB, c), F32)
            for s in range(SUB):
                r = blk * SUB + s
                a_s = jnp.sum(qb_ * k_row(r) * jnp.exp(bb_ - b_row(r)), axis=-1, keepdims=True)
                acc = jnp.where(lane == r, a_s, acc)
            blocks.append(jnp.where(lane <= sub_row + blk * SUB, acc, 0.0))
        a_mat = a_mat + jnp.concatenate(blocks, axis=0)
        o = o + jnp.dot(a_mat.astype(BF16), vb, preferred_element_type=F32)

    b_last = b[c - 1:c, :]
    kt = (k * jnp.exp(b_last - b)).astype(BF16)
    s_new = _row_to_col(jnp.exp(b_last)) * s0 + lax.dot_general(kt, vb, TN_DIMS, preferred_element_type=F32)
    return o, s_new


def _hg_gates(hq, hf, lb):
    sig = _sigmoid(hf)
    lf = jnp.log(lb + (1.0 - lb) * sig)
    k = (1.0 - lb) * _sigmoid(-hf)
    return _silu(hq), k, lf


def _hg_prompt_body(hq_ref, hf_ref, hi_ref, hg_ref, lb_ref, gn_ref, o_ref, s_ref, k_scr, b_scr, *, chunk):
    t = hq_ref.shape[0]
    n_heads = hq_ref.shape[1] // HD
    gn = gn_ref[...]

    def step(ci, states):
        rows = pl.ds(pl.multiple_of(ci * chunk, chunk), chunk)
        new_states = []
        for h in range(n_heads):
            cs = slice(h * HD, (h + 1) * HD)
            q, k, lf = _hg_gates(hq_ref[rows, cs], hf_ref[rows, cs], lb_ref[:, cs])
            o, s = _hg_chunk(q, k, hi_ref[rows, cs], lf, states[h], k_scr.at[h], b_scr.at[h])
            o_ref[rows, cs] = (_rms(o, gn) * _silu(hg_ref[rows, cs])).astype(o_ref.dtype)
            new_states.append(s)
        return tuple(new_states)

    states = lax.fori_loop(0, t // chunk, step, tuple(jnp.zeros((HD, HD), F32) for _ in range(n_heads)))
    for h in range(n_heads):
        s_ref[0, h] = states[h]


def _hg_prompt(z, lb, gn, nb, t, offs, heads_per_step=4):
    hw = heads_per_step * HD
    qb, fb, ib, gb = (offs[n] // hw for n in ("hq", "hf", "hi", "hg"))
    chunk = min(128, t)
    col = lambda base: pl.BlockSpec((t, hw), lambda b, h: (b, base + h))
    return pl.pallas_call(
        functools.partial(_hg_prompt_body, chunk=chunk),
        grid=(nb, HG_H // heads_per_step),
        in_specs=[col(qb), col(fb), col(ib), col(gb),
                  pl.BlockSpec((1, hw), lambda b, h: (0, h)),
                  pl.BlockSpec((1, HD), lambda b, h: (0, 0))],
        out_specs=[pl.BlockSpec((t, hw), lambda b, h: (b, h)),
                   pl.BlockSpec((1, heads_per_step, HD, HD), lambda b, h: (b, h, 0, 0))],
        out_shape=[jax.ShapeDtypeStruct((nb * t, HG_H * HD), BF16),
                   jax.ShapeDtypeStruct((nb, HG_H, HD, HD), F32)],
        scratch_shapes=[pltpu.VMEM((heads_per_step, chunk, HD), F32)] * 2,
        compiler_params=_cparams("parallel", "parallel"),
        name="hgrn2_prompt",
    )(z, z, z, z, lb, gn)


def _hg_sample_body(hq_ref, hf_ref, hi_ref, hg_ref, lb_ref, gn_ref, s0_ref, o_ref, s_ref, *, n_tok):
    valid = lax.broadcasted_iota(jnp.int32, (PAD_L, 1), 0) < n_tok
    gn = gn_ref[...]
    for i in range(hq_ref.shape[0]):
        for h in range(HG_H):
            cs = slice(h * HD, (h + 1) * HD)
            q, k, lf = _hg_gates(hq_ref[i, :, cs], hf_ref[i, :, cs], lb_ref[:, cs])
            k = jnp.where(valid, k, 0.0)
            lf = jnp.where(valid, lf, 0.0)
            o, s = _hg_chunk(q, k, hi_ref[i, :, cs], lf, s0_ref[0, i, h])
            s_ref[i, h] = s
            o_ref[i, :, cs] = (_rms(o, gn) * _silu(hg_ref[i, :, cs])).astype(o_ref.dtype)


def _hg_sample(zs, lb, gn, state, layer, offs, n_tok):
    db = zs.shape[0]
    w = HG_H * HD
    ns = 2 if db % 2 == 0 else 1
    col = lambda name: pl.BlockSpec((ns, PAD_L, w), lambda b: (b, 0, offs[name] // w))
    return pl.pallas_call(
        functools.partial(_hg_sample_body, n_tok=n_tok),
        grid=(db // ns,),
        in_specs=[col("hq"), col("hf"), col("hi"), col("hg"),
                  pl.BlockSpec((1, w), lambda b: (0, 0)),
                  pl.BlockSpec((1, HD), lambda b: (0, 0)),
                  pl.BlockSpec((1, ns, HG_H, HD, HD), lambda b: (layer, b, 0, 0, 0))],
        out_specs=[pl.BlockSpec((ns, PAD_L, w), lambda b: (b, 0, 0)),
                   pl.BlockSpec((ns, HG_H, HD, HD), lambda b: (b, 0, 0, 0))],
        out_shape=[jax.ShapeDtypeStruct((db, PAD_L, w), BF16),
                   jax.ShapeDtypeStruct((db, HG_H, HD, HD), F32)],
        compiler_params=_cparams("parallel"),
        name="hgrn2_sample",
    )(zs, zs, zs, zs, lb, gn, state)


def _lru_gates(xc, wa_ref, ba, wx_ref, bx, lam):
    nblk = xc.shape[1] // HD
    xb = xc.astype(BF16)
    pa = [jnp.dot(xb[:, i * HD:(i + 1) * HD], wa_ref[i], preferred_element_type=F32) for i in range(nblk)]
    px = [jnp.dot(xb[:, i * HD:(i + 1) * HD], wx_ref[i], preferred_element_type=F32) for i in range(nblk)]
    r = _sigmoid(jnp.concatenate(pa, axis=1) + ba)
    ig = _sigmoid(jnp.concatenate(px, axis=1) + bx)
    log_a = -LRU_C * r * _softplus(-lam)
    a = jnp.exp(log_a)
    u = jnp.sqrt(-jnp.tanh(log_a) * (1.0 + a * a)) * (ig * xc)
    return a, u


def _lru_prompt_body(lx_ref, ly_ref, cw_ref, cb_ref, wa_ref, ba_ref, wx_ref, bx_ref, lam_ref,
                     o_ref, h_ref, *, rt):
    t, wt = lx_ref.shape
    cw = cw_ref[...]
    sub_row = lax.broadcasted_iota(jnp.int32, (SUB, 1), 0)
    pos = lax.broadcasted_iota(jnp.int32, (rt, 1), 0) & (SUB - 1)

    def tile(ti, carry):
        h, tail = carry
        rows = pl.ds(pl.multiple_of(ti * rt, rt), rt)
        x = lx_ref[rows, :]
        xc = cb_ref[...] + x * cw[CONV_W - 1:CONV_W, :]
        for d in range(1, CONV_W):
            xr = pltpu.roll(x, d, 0)
            head = jnp.where(sub_row < d, pltpu.roll(tail, d, 0), xr[0:SUB])
            xs = jnp.concatenate([head, xr[SUB:]], axis=0) if rt > SUB else head
            xc = xc + xs * cw[CONV_W - 1 - d:CONV_W - d, :]
        a, u = _lru_gates(xc, wa_ref, ba_ref[...], wx_ref, bx_ref[...], lam_ref[...])
        d = 1
        while d < SUB:
            keep = pos >= d
            u = a * jnp.where(keep, pltpu.roll(u, d, 0), 0.0) + u
            a = a * jnp.where(keep, pltpu.roll(a, d, 0), 1.0)
            d *= 2
        gelu = _gelu_tanh(ly_ref[rows, :])
        for g in range(rt // SUB):
            sl = slice(g * SUB, (g + 1) * SUB)
            hs = a[sl] * h + u[sl]
            h = hs[SUB - 1:SUB, :]
            o_ref[pl.ds(pl.multiple_of(ti * rt, rt) + g * SUB, SUB), :] = (hs * gelu[sl]).astype(o_ref.dtype)
        return h, x[rt - SUB:rt, :]

    h, _ = lax.fori_loop(0, t // rt, tile, (jnp.zeros((1, wt), F32), jnp.zeros((SUB, wt), F32)))
    h_ref[0] = h


def _lru_prompt(z, p, nb, t, offs):
    w = p["lru_conv_b"].shape[1]
    wt = SEG
    nblk = wt // HD
    rt = min(128, t)
    xb, yb = offs["lx"] // wt, offs["ly"] // wt
    vec = pl.BlockSpec((1, wt), lambda b, c: (0, c))
    gate_w = pl.BlockSpec((nblk, HD, HD), lambda b, c: (c, 0, 0))
    return pl.pallas_call(
        functools.partial(_lru_prompt_body, rt=rt),
        grid=(nb, w // wt),
        in_specs=[pl.BlockSpec((t, wt), lambda b, c: (b, xb + c)),
                  pl.BlockSpec((t, wt), lambda b, c: (b, yb + c)),
                  pl.BlockSpec((CONV_W, wt), lambda b, c: (0, c)),
                  vec, gate_w, vec, gate_w, vec, vec],
        out_specs=[pl.BlockSpec((t, wt), lambda b, c: (b, c)),
                   pl.BlockSpec((1, 1, wt), lambda b, c: (b, 0, c))],
        out_shape=[jax.ShapeDtypeStruct((nb * t, w), BF16),
                   jax.ShapeDtypeStruct((nb, 1, w), F32)],
        compiler_params=_cparams("parallel", "parallel"),
        name="lru_prompt",
    )(z, z, p["lru_conv_w"], p["lru_conv_b"], p["lru_w_a"], p["lru_b_a"], p["lru_w_x"], p["lru_b_x"],
      p["lru_lambda"])


def _lru_sample_body(*refs, n_tok):
    lx = refs[0:n_tok]
    ly = refs[n_tok:2 * n_tok]
    buf = refs[2 * n_tok:2 * n_tok + CONV_W - 1]
    h0_ref, cw_ref, cb_ref, wa_ref, ba_ref, wx_ref, bx_ref, lam_ref = refs[2 * n_tok + CONV_W - 1:2 * n_tok + CONV_W + 7]
    outs = refs[2 * n_tok + CONV_W + 7:]
    o_refs, h_ref = outs[0:n_tok], outs[n_tok]
    cw = cw_ref[...]
    xp = [r[0] for r in buf] + [r[...] for r in lx]
    h = h0_ref[0]
    for t in range(n_tok):
        xc = cb_ref[...]
        for j in range(CONV_W):
            xc = xc + xp[t + j] * cw[j:j + 1, :]
        a, u = _lru_gates(xc, wa_ref, ba_ref[...], wx_ref, bx_ref[...], lam_ref[...])
        h = a * h + u
        o_refs[t][...] = (h * _gelu_tanh(ly[t][...])).astype(o_refs[t].dtype)
    h_ref[...] = h


def _lru_sample(zs2, p, state_lru, state_conv, layer, offs, n_tok, nz):
    db = zs2.shape[0]
    w = p["lru_conv_b"].shape[1]
    wt = SEG
    nblk = wt // HD
    sc = state_conv.reshape(state_conv.shape[0], db, (CONV_W - 1) * w)
    tok = lambda name, l: pl.BlockSpec((db, wt), lambda c: (0, (l * nz + offs[name]) // wt + c))
    vec = pl.BlockSpec((1, wt), lambda c: (0, c))
    gate_w = pl.BlockSpec((nblk, HD, HD), lambda c: (c, 0, 0))
    in_specs = ([tok("lx", l) for l in range(n_tok)] + [tok("ly", l) for l in range(n_tok)]
                + [pl.BlockSpec((1, db, wt), lambda c, j=j: (layer, 0, j * (w // wt) + c)) for j in range(CONV_W - 1)]
                + [pl.BlockSpec((1, db, wt), lambda c: (layer, 0, c)),
                   pl.BlockSpec((CONV_W, wt), lambda c: (0, c)),
                   vec, gate_w, vec, gate_w, vec, vec])
    out_spec = pl.BlockSpec((db, wt), lambda c: (0, c))
    outs = pl.pallas_call(
        functools.partial(_lru_sample_body, n_tok=n_tok),
        grid=(w // wt,),
        in_specs=in_specs,
        out_specs=[out_spec] * (n_tok + 1),
        out_shape=[jax.ShapeDtypeStruct((db, w), BF16)] * n_tok + [jax.ShapeDtypeStruct((db, w), F32)],
        compiler_params=_cparams("parallel"),
        name="lru_sample",
    )(*([zs2] * (2 * n_tok) + [sc] * (CONV_W - 1)
        + [state_lru, p["lru_conv_w"], p["lru_conv_b"], p["lru_w_a"], p["lru_b_a"], p["lru_w_x"], p["lru_b_x"],
           p["lru_lambda"]]))
    return jnp.stack(outs[:n_tok], axis=1).reshape(db * n_tok, w), outs[n_tok]


def _xattn_prompt_body(q_ref, k_ref, v_ref, o_ref):
    scale = HD ** -0.5
    for h in range(X_H):
        cs = slice(h * HD, (h + 1) * HD)
        q = (q_ref[:, cs] * scale).astype(BF16)
        s = lax.dot_general(q, k_ref[0, :, cs].astype(BF16), NT_DIMS, preferred_element_type=F32)
        p = jnp.exp(s - jnp.max(s, axis=-1, keepdims=True))
        l = jnp.sum(p, axis=-1, keepdims=True)
        o = jnp.dot(p.astype(BF16), v_ref[0, :, cs].astype(BF16), preferred_element_type=F32) / l
        o_ref[:, cs] = o.astype(o_ref.dtype)


def _xattn_prompt(q, mk, mv, nb, t):
    w = X_H * HD
    m = mk.shape[1]
    tq = min(512, t)
    return pl.pallas_call(
        _xattn_prompt_body,
        grid=(nb, t // tq),
        in_specs=[pl.BlockSpec((tq, w), lambda b, i: (b * (t // tq) + i, 0)),
                  pl.BlockSpec((1, m, w), lambda b, i: (b, 0, 0)),
                  pl.BlockSpec((1, m, w), lambda b, i: (b, 0, 0))],
        out_specs=pl.BlockSpec((tq, w), lambda b, i: (b * (t // tq) + i, 0)),
        out_shape=jax.ShapeDtypeStruct((nb * t, w), BF16),
        compiler_params=_cparams("parallel", "arbitrary"),
        name="xattn_prompt",
    )(q, mk, mv)


def _xattn_sample_body(q_ref, k_ref, v_ref, o_ref, *, n_seq):
    scale = HD ** -0.5
    for i in range(n_seq):
        q = (q_ref[i] * scale).astype(BF16)
        s = lax.dot_general(q, k_ref[0, i].astype(BF16), NT_DIMS, preferred_element_type=F32)
        r = lax.broadcasted_iota(jnp.int32, s.shape, 0)
        c = lax.broadcasted_iota(jnp.int32, s.shape, 1)
        s = jnp.where((r & (X_H - 1)) == (c & (X_H - 1)), s, NEG_INF)
        p = jnp.exp(s - jnp.max(s, axis=-1, keepdims=True))
        l = jnp.sum(p, axis=-1, keepdims=True)
        o = jnp.dot(p.astype(BF16), v_ref[0, i].astype(BF16), preferred_element_type=F32) / l
        o_ref[i] = o.astype(o_ref.dtype)


def _xattn_sample(q, cache_mk, cache_mv, layer):
    db, nrow, _ = q.shape
    mx = cache_mk.shape[2]
    n_seq = min(8, db)
    kv = pl.BlockSpec((1, n_seq, mx, HD), lambda b: (layer, b, 0, 0))
    return pl.pallas_call(
        functools.partial(_xattn_sample_body, n_seq=n_seq),
        grid=(db // n_seq,),
        in_specs=[pl.BlockSpec((n_seq, nrow, HD), lambda b: (b, 0, 0)), kv, kv],
        out_specs=pl.BlockSpec((n_seq, nrow, HD), lambda b: (b, 0, 0)),
        out_shape=jax.ShapeDtypeStruct((db, nrow, HD), BF16),
        compiler_params=_cparams("parallel"),
        name="xattn_sample",
    )(q, cache_mk, cache_mv)


def _segment_offsets(d):
    fox_w, hg_w, lru_w = FOX_H * HD, HG_H * HD, HG_H * HD
    names = [("fq", fox_w), ("fk", fox_w), ("fv", fox_w), ("hq", hg_w), ("hf", hg_w), ("hi", hg_w), ("hg", hg_w),
             ("lx", lru_w), ("ly", lru_w), ("ga", d), ("gb", d), ("gc", d), ("ff", FF_SEG)]
    offs, o = {}, 0
    for n, wdt in names:
        offs[n] = o
        o += wdt
    return offs, o


def _rearrange_body(main_ref, extra_ref, o_ref, *, n_plain, n_shift):
    c = pl.program_id(2)
    tr, tw = o_ref.shape

    @pl.when(c < n_plain)
    def _():
        o_ref[...] = main_ref[...].astype(o_ref.dtype)

    @pl.when((c >= n_plain) & (c < n_shift))
    def _():
        y = jnp.concatenate([main_ref[...], extra_ref[...]], axis=1)
        o_ref[...] = pltpu.roll(y, tw + LANES - FOX_H, 1)[:, 0:tw].astype(o_ref.dtype)

    @pl.when(c == n_shift)
    def _():
        lane = lax.broadcasted_iota(jnp.int32, (tr, LANES), 1)
        ff = jnp.where(lane < FOX_H, extra_ref[...], 0.0)
        o_ref[...] = jnp.concatenate([ff, jnp.zeros((tr, tw - LANES), F32)], axis=1).astype(o_ref.dtype)

    @pl.when(c > n_shift)
    def _():
        o_ref[...] = jnp.zeros_like(o_ref)


def _rearrange_w_in(w_in, nz):
    depth, d, n_in = w_in.shape
    fw3 = 3 * FOX_H * HD
    off_ff = n_in - FOX_H
    tw = next(t for t in (1024, 512, 256, 128) if fw3 % t == 0 and off_ff % t == 0 and nz % t == 0)
    tr = _row_tile(d, cap=512)
    n_plain, n_shift = fw3 // tw, off_ff // tw
    per = tw // LANES

    def main_map(l, r, c):
        return (l, r, jnp.minimum(c, n_shift - 1))

    def extra_map(l, r, c):
        shifted = (c >= n_plain) & (c < n_shift)
        return (l, r, jnp.where(shifted, (c + 1) * per, fw3 // LANES))

    return pl.pallas_call(
        functools.partial(_rearrange_body, n_plain=n_plain, n_shift=n_shift),
        grid=(depth, d // tr, nz // tw),
        in_specs=[pl.BlockSpec((None, tr, tw), main_map), pl.BlockSpec((None, tr, LANES), extra_map)],
        out_specs=pl.BlockSpec((None, tr, tw), lambda l, r, c: (l, r, c)),
        out_shape=jax.ShapeDtypeStruct((depth, d, nz), BF16),
        compiler_params=_cparams("parallel", "parallel", "arbitrary"),
        name="rearrange_w_in",
    )(w_in, w_in)


def _pad_tokens(x, db, n_tok):
    return jnp.pad(x.reshape(db, n_tok, x.shape[1]), ((0, 0), (0, PAD_L - n_tok), (0, 0)))


def kernel(x_prompt, x_sample, mem_prompt, cache_k, cache_v, cache_logf, state_hgrn, state_lru, state_conv, cache_mem_k, cache_mem_v, page_table, ffn1_norm, ffn1_w_gate, ffn1_w_up, ffn1_w_down, mix_norm, w_in, fox_b_f, fox_q_norm, fox_k_norm, hg_lb_param, hg_out_norm, lru_conv_w, lru_conv_b, lru_w_a, lru_b_a, lru_w_x, lru_b_x, lru_lambda, w_br_fox, w_br_hg, w_br_lru, w_out, xa_norm, xa_mem_norm, xa_w_q, xa_w_kv, xa_q_norm, xa_k_norm, xa_w_o, ffn2_norm, ffn2_w_gate, ffn2_w_up, ffn2_w_down):
    nb, t, d = x_prompt.shape
    db, n_tok, _ = x_sample.shape
    depth = w_in.shape[0]
    n_mem = mem_prompt.shape[1]
    np_rows = nb * t
    fw, xw = FOX_H * HD, X_H * HD
    lw = lru_conv_b.shape[1]
    assert n_tok <= PAD_L and n_tok >= CONV_W - 1 and t >= CONV_W - 1
    offs, nz = _segment_offsets(d)

    sm = jax.nn.softmax(hg_lb_param.astype(F32), axis=0)
    hg_lb = jnp.cumsum(sm, axis=0) - sm[0]

    x = jnp.concatenate([x_prompt.reshape(np_rows, d), x_sample.reshape(db * n_tok, d)], axis=0)
    mem = mem_prompt.reshape(nb * n_mem, d)
    cmk = cache_mem_k.reshape(depth, db, n_mem * X_H, HD)
    cmv = cache_mem_v.reshape(depth, db, n_mem * X_H, HD)

    cache_sums = _fox_cache_sums(cache_logf)
    w_in_r = _rearrange_w_in(w_in, nz)
    w16 = {name: w.astype(BF16) for name, w in dict(
        ffn1_g=ffn1_w_gate, ffn1_u=ffn1_w_up, ffn1_d=ffn1_w_down, ffn2_g=ffn2_w_gate, ffn2_u=ffn2_w_up,
        ffn2_d=ffn2_w_down, br_fox=w_br_fox, br_hg=w_br_hg, br_lru=w_br_lru, out=w_out,
        xa_q=xa_w_q, xa_kv=xa_w_kv, xa_o=xa_w_o).items()}

    outs = {n: [] for n in ("pk", "pv", "plf", "ps", "ph", "pc", "pmk", "pmv", "sk", "sv", "slf", "ss", "sh", "sc")}
    for l in range(depth):
        bf = lambda a: a[l].astype(BF16)
        row = lambda a: a[l].reshape(1, -1).astype(F32)
        lru_p = dict(lru_conv_w=lru_conv_w[l], lru_conv_b=row(lru_conv_b), lru_w_a=bf(lru_w_a), lru_b_a=row(lru_b_a),
                     lru_w_x=bf(lru_w_x), lru_b_x=row(lru_b_x), lru_lambda=row(lru_lambda))
        fox_bias = jnp.pad(row(fox_b_f), ((0, 0), (0, LANES - FOX_H)))
        head_gain = jnp.concatenate([jnp.tile(row(fox_q_norm), (1, FOX_H)), jnp.tile(row(fox_k_norm), (1, FOX_H)),
                                     jnp.ones((1, nz - 2 * fw), F32)], axis=1)

        x = _ffn(x, ffn1_norm[l], w16["ffn1_g"], w16["ffn1_u"], w16["ffn1_d"], l)
        z = _norm_matmul(x, mix_norm[l], w_in_r, l, head_gain, n_norm_cols=2 * fw)
        zs = _pad_tokens(z[np_rows:], db, n_tok)

        plf, c_row = _fox_prep(z, fox_bias, nb, t, offs["ff"])
        of_p = _fox_prompt(z, c_row, nb, t, offs)
        slf, cum = _fox_gates(zs, fox_bias, offs["ff"], n_tok)
        slf = slf.reshape(db, PAD_L, LANES)
        cum = cum.reshape(db, PAD_L, LANES)[:, :, :FOX_H]
        heads = lambda name, rows: zs[:, :rows, offs[name]:offs[name] + fw].reshape(db, rows * FOX_H, HD)
        of_s = _fox_sample(heads("fq", n_tok), heads("fk", PAD_L), heads("fv", PAD_L),
                           cum[:, :n_tok].reshape(db, n_tok * FOX_H, 1), cum.reshape(db, 1, PAD_L * FOX_H),
                           cache_k, cache_v, cache_sums, page_table, l).reshape(db * n_tok, fw)
        oh_p, ps = _hg_prompt(z, row(hg_lb), row(hg_out_norm), nb, t, offs)
        oh_s, ss = _hg_sample(zs, row(hg_lb), row(hg_out_norm), state_hgrn, l, offs, n_tok)
        ol_p, ph = _lru_prompt(z, lru_p, nb, t, offs)
        ol_s, sh = _lru_sample(zs.reshape(db, PAD_L * nz), lru_p, state_lru, state_conv, l, offs, n_tok, nz)

        unpad = lambda a: a[:, :n_tok].reshape(db * n_tok, a.shape[2])
        o_fox = jnp.concatenate([of_p, of_s], axis=0)
        o_hg = jnp.concatenate([oh_p, unpad(oh_s)], axis=0)
        o_lru = jnp.concatenate([ol_p, ol_s], axis=0)
        x = _merge(o_fox, o_hg, o_lru, w16["br_fox"], w16["br_hg"], w16["br_lru"], z, offs["ga"], w16["out"], x, l)

        xq = _norm_matmul(x, xa_norm[l], w16["xa_q"], l, jnp.tile(row(xa_q_norm), (1, X_H)), n_norm_cols=xw)
        kv_gain = jnp.concatenate([jnp.tile(row(xa_k_norm), (1, X_H)), jnp.ones((1, xw), F32)], axis=1)
        mkv = _norm_matmul(mem, xa_mem_norm[l], w16["xa_kv"], l, kv_gain, n_norm_cols=xw)
        mk, mv = mkv[:, :xw].reshape(nb, n_mem, xw), mkv[:, xw:].reshape(nb, n_mem, xw)
        ox_p = _xattn_prompt(xq[:np_rows], mk, mv, nb, t)
        ox_s = _xattn_sample(xq[np_rows:].reshape(db, n_tok * X_H, HD), cmk, cmv, l).reshape(db * n_tok, xw)
        x = _matmul_residual(jnp.concatenate([ox_p, ox_s], axis=0), w16["xa_o"], x, l)

        x = _ffn(x, ffn2_norm[l], w16["ffn2_g"], w16["ffn2_u"], w16["ffn2_d"], l)

        zp, zsm = z[:np_rows], z[np_rows:]
        outs["pk"].append(zp[:, offs["fk"]:offs["fk"] + fw])
        outs["pv"].append(zp[:, offs["fv"]:offs["fv"] + fw])
        outs["plf"].append(plf)
        outs["ps"].append(ps)
        outs["ph"].append(ph.reshape(nb, lw))
        outs["pc"].append(zp[:, offs["lx"]:offs["lx"] + lw].reshape(nb, t, lw)[:, t - (CONV_W - 1):])
        outs["pmk"].append(mk.reshape(nb, n_mem, X_H, HD))
        outs["pmv"].append(mv.reshape(nb, n_mem, X_H, HD))
        outs["sk"].append(zsm[:, offs["fk"]:offs["fk"] + fw].reshape(db, n_tok, FOX_H, HD))
        outs["sv"].append(zsm[:, offs["fv"]:offs["fv"] + fw].reshape(db, n_tok, FOX_H, HD))
        outs["slf"].append(slf[:, :n_tok, :FOX_H])
        outs["ss"].append(ss)
        outs["sh"].append(sh)
        outs["sc"].append(zsm[:, offs["lx"]:offs["lx"] + lw].reshape(db, n_tok, lw)[:, n_tok - (CONV_W - 1):])

    st = lambda n: jnp.stack(outs[n])
    return (x[:np_rows].reshape(nb, t, d), x[np_rows:].reshape(db, n_tok, d),
            st("pk").reshape(depth, nb, t, FOX_H, HD), st("pv").reshape(depth, nb, t, FOX_H, HD),
            st("plf"), st("ps"), st("ph"), st("pc"), st("pmk"), st("pmv"),
            st("sk"), st("sv"), st("slf"), st("ss"), st("sh"), st("sc"))
```

```python
import functools

import jax
import jax.numpy as jnp
from jax import lax
from jax.experimental import pallas as pl
from jax.experimental.pallas import tpu as pltpu

F32 = jnp.float32
BF16 = jnp.bfloat16
EPS = 1e-6
NEG_INF = -1e30
HD = 128
FOX_H = 8
HG_H = 8
X_H = 4
LRU_C = 8.0
CONV_W = 4
SUB = 8
LANES = 128
SEG = 512
FF_SEG = 1024
VMEM_LIMIT = 56 * 1024 * 1024
ROW_TILES = (768, 512, 256, 128, 64, 32, 16, 8)
PAD_L = 8
PAGE_GROUP = 16

NT_DIMS = (((1,), (1,)), ((), ()))
TN_DIMS = (((0,), (0,)), ((), ()))


def _cparams(*sem):
    return pltpu.CompilerParams(dimension_semantics=sem, vmem_limit_bytes=VMEM_LIMIT)


def _row_tile(n, cap=ROW_TILES[0]):
    return next(t for t in ROW_TILES if t <= cap and n % t == 0)


def _col_tile(n, cands, also_divides=0):
    return next(t for t in cands if n % t == 0 and also_divides % t == 0)


def _rms(x, g):
    return x * lax.rsqrt(jnp.mean(x * x, axis=-1, keepdims=True) + EPS) * g


def _sigmoid(x):
    return jax.nn.sigmoid(x)


def _silu(x):
    return x * jax.nn.sigmoid(x)


def _log_sigmoid(x):
    return jnp.minimum(x, 0.0) - jnp.log1p(jnp.exp(-jnp.abs(x)))


def _softplus(x):
    return jnp.maximum(x, 0.0) + jnp.log1p(jnp.exp(-jnp.abs(x)))


def _gelu_tanh(x):
    return 0.5 * x * (1.0 + jnp.tanh(0.7978845608028654 * (x + 0.044715 * (x * x * x))))


def _cumsum_lanes(x, reverse=False):
    lane = lax.broadcasted_iota(jnp.int32, x.shape, 1)
    d = 1
    while d < LANES:
        if reverse:
            x = x + jnp.where(lane < LANES - d, pltpu.roll(x, LANES - d, 1), 0.0)
        else:
            x = x + jnp.where(lane >= d, pltpu.roll(x, d, 1), 0.0)
        d *= 2
    return x


def _cumsum_rows(x):
    c = x.shape[0]
    row = lax.broadcasted_iota(jnp.int32, (c, 1), 0)
    d = 1
    while d < c:
        x = x + jnp.where(row >= d, pltpu.roll(x, d, 0), 0.0)
        d *= 2
    return x


def _row_to_col(r):
    return jnp.broadcast_to(r, (SUB, r.shape[1])).T[:, 0:1]


def _ffn_body(x_ref, g_ref, wg_ref, wu_ref, wd_ref, o_ref, h_ref, acc_ref):
    j = pl.program_id(1)

    @pl.when(j == 0)
    def _():
        h_ref[...] = _rms(x_ref[...], g_ref[...]).astype(BF16)
        acc_ref[...] = jnp.zeros_like(acc_ref)

    h = h_ref[...]
    gt = jnp.dot(h, wg_ref[...], preferred_element_type=F32)
    ut = jnp.dot(h, wu_ref[...], preferred_element_type=F32)
    a = (_silu(gt) * ut).astype(BF16)
    acc_ref[...] += jnp.dot(a, wd_ref[...], preferred_element_type=F32)

    @pl.when(j == pl.num_programs(1) - 1)
    def _():
        o_ref[...] = x_ref[...] + 0.5 * acc_ref[...]


def _ffn(x, g, wg, wu, wd, layer, tf=512):
    n, d = x.shape
    f = wg.shape[2]
    tm = _row_tile(n)
    tf = min(tf, f)
    return pl.pallas_call(
        _ffn_body,
        grid=(n // tm, f // tf),
        in_specs=[
            pl.BlockSpec((tm, d), lambda i, j: (i, 0)),
            pl.BlockSpec((1, d), lambda i, j: (0, 0)),
            pl.BlockSpec((None, d, tf), lambda i, j: (layer, 0, j)),
            pl.BlockSpec((None, d, tf), lambda i, j: (layer, 0, j)),
            pl.BlockSpec((None, tf, d), lambda i, j: (layer, j, 0)),
        ],
        out_specs=pl.BlockSpec((tm, d), lambda i, j: (i, 0)),
        out_shape=jax.ShapeDtypeStruct((n, d), F32),
        scratch_shapes=[pltpu.VMEM((tm, d), BF16), pltpu.VMEM((tm, d), F32)],
        compiler_params=_cparams("parallel", "arbitrary"),
        name="ffn",
    )(x, g.reshape(1, d), wg, wu, wd)


def _nmm_body(x_ref, g_ref, w_ref, gn_ref, o_ref, h_ref, *, n_norm_tiles):
    j = pl.program_id(1)

    @pl.when(j == 0)
    def _():
        h_ref[...] = _rms(x_ref[...], g_ref[...]).astype(BF16)

    y = jnp.dot(h_ref[...], w_ref[...], preferred_element_type=F32)
    if n_norm_tiles == 0:
        o_ref[...] = y
    else:
        @pl.when(j < n_norm_tiles)
        def _():
            tn = y.shape[1]
            parts = [_rms(y[:, c * HD:(c + 1) * HD], gn_ref[:, c * HD:(c + 1) * HD]) for c in range(tn // HD)]
            o_ref[...] = jnp.concatenate(parts, axis=1)

        @pl.when(j >= n_norm_tiles)
        def _():
            o_ref[...] = y


def _norm_matmul(x, g, w, layer, head_gain=None, n_norm_cols=0):
    n, d = x.shape
    nout = w.shape[2]
    tm = _row_tile(n)
    tn = _col_tile(nout, (2048, 1024, 512, 256, 128), n_norm_cols)
    if head_gain is None:
        head_gain = jnp.ones((1, nout), F32)
    return pl.pallas_call(
        functools.partial(_nmm_body, n_norm_tiles=n_norm_cols // tn),
        grid=(n // tm, nout // tn),
        in_specs=[
            pl.BlockSpec((tm, d), lambda i, j: (i, 0)),
            pl.BlockSpec((1, d), lambda i, j: (0, 0)),
            pl.BlockSpec((None, d, tn), lambda i, j: (layer, 0, j)),
            pl.BlockSpec((1, tn), lambda i, j: (0, j)),
        ],
        out_specs=pl.BlockSpec((tm, tn), lambda i, j: (i, j)),
        out_shape=jax.ShapeDtypeStruct((n, nout), F32),
        scratch_shapes=[pltpu.VMEM((tm, d), BF16)],
        compiler_params=_cparams("parallel", "arbitrary"),
        name="norm_matmul",
    )(x, g.reshape(1, d), w, head_gain)


def _mmres_body(a_ref, w_ref, r_ref, o_ref):
    o_ref[...] = r_ref[...] + jnp.dot(a_ref[...], w_ref[...], preferred_element_type=F32)


def _matmul_residual(a, w, res, layer):
    n, k = a.shape
    nout = w.shape[2]
    tm = _row_tile(n)
    tn = _col_tile(nout, (2048, 1024, 512, 256, 128))
    return pl.pallas_call(
        _mmres_body,
        grid=(n // tm, nout // tn),
        in_specs=[
            pl.BlockSpec((tm, k), lambda i, j: (i, 0)),
            pl.BlockSpec((None, k, tn), lambda i, j: (layer, 0, j)),
            pl.BlockSpec((tm, tn), lambda i, j: (i, j)),
        ],
        out_specs=pl.BlockSpec((tm, tn), lambda i, j: (i, j)),
        out_shape=jax.ShapeDtypeStruct((n, nout), F32),
        compiler_params=_cparams("parallel", "arbitrary"),
        name="matmul_residual",
    )(a, w, res)


def _merge_body(of_ref, oh_ref, ol_ref, wf_ref, wh_ref, wl_ref, ga_ref, gb_ref, gc_ref, wo_ref, x_ref, o_ref):
    j = pl.program_id(1)
    y = _sigmoid(ga_ref[...]) * jnp.dot(of_ref[...], wf_ref[...], preferred_element_type=F32)
    y = y + _sigmoid(gb_ref[...]) * jnp.dot(oh_ref[...], wh_ref[...], preferred_element_type=F32)
    y = y + _sigmoid(gc_ref[...]) * jnp.dot(ol_ref[...], wl_ref[...], preferred_element_type=F32)
    part = jnp.dot(y.astype(BF16), wo_ref[...], preferred_element_type=F32)

    @pl.when(j == 0)
    def _():
        o_ref[...] = x_ref[...] + part

    @pl.when(j > 0)
    def _():
        o_ref[...] += part


def _merge(o_fox, o_hg, o_lru, wf, wh, wl, z, gate_off, w_out, x, layer, tn=SEG):
    n, k = o_fox.shape
    d = wf.shape[2]
    tm = _row_tile(n, cap=512)
    tn = min(tn, d)
    gblk = gate_off // tn
    dblk = d // tn
    act = pl.BlockSpec((tm, k), lambda i, j: (i, 0))
    wsp = pl.BlockSpec((None, k, tn), lambda i, j: (layer, 0, j))
    full = pl.BlockSpec((tm, d), lambda i, j: (i, 0))
    return pl.pallas_call(
        _merge_body,
        grid=(n // tm, dblk),
        in_specs=[act, act, act, wsp, wsp, wsp,
                  pl.BlockSpec((tm, tn), lambda i, j: (i, gblk + j)),
                  pl.BlockSpec((tm, tn), lambda i, j: (i, gblk + dblk + j)),
                  pl.BlockSpec((tm, tn), lambda i, j: (i, gblk + 2 * dblk + j)),
                  pl.BlockSpec((None, tn, d), lambda i, j: (layer, j, 0)),
                  full],
        out_specs=full,
        out_shape=jax.ShapeDtypeStruct((n, d), F32),
        compiler_params=_cparams("parallel", "arbitrary"),
        name="merge",
    )(o_fox, o_hg, o_lru, wf, wh, wl, z, z, z, w_out, x)


def _fox_prep_body(zf_ref, bias_ref, lf_ref, c_ref):
    t = zf_ref.shape[0]
    lf = _log_sigmoid(zf_ref[:, 0:LANES] + bias_ref[...])
    lf_ref[0] = lf[:, 0:FOX_H]
    x = lf.T[0:FOX_H, :]
    carry = jnp.zeros((FOX_H, 1), F32)
    for blk in range(t // LANES):
        cs = _cumsum_lanes(x[:, blk * LANES:(blk + 1) * LANES]) + carry
        c_ref[0, :, blk * LANES:(blk + 1) * LANES] = cs
        carry = cs[:, LANES - 1:LANES]


def _fox_prep(z, bias, nb, t, ff_off):
    return pl.pallas_call(
        _fox_prep_body,
        grid=(nb,),
        in_specs=[pl.BlockSpec((t, SEG), lambda b: (b, ff_off // SEG)),
                  pl.BlockSpec((1, LANES), lambda b: (0, 0))],
        out_specs=[pl.BlockSpec((1, t, FOX_H), lambda b: (b, 0, 0)),
                   pl.BlockSpec((1, FOX_H, t), lambda b: (b, 0, 0))],
        out_shape=[jax.ShapeDtypeStruct((nb, t, FOX_H), F32),
                   jax.ShapeDtypeStruct((nb, FOX_H, t), F32)],
        compiler_params=_cparams("parallel"),
        name="fox_prep",
    )(z, bias)


def _fox_prompt_body(q_ref, k_ref, v_ref, c_ref, o_ref, *, tq):
    t = q_ref.shape[0]
    scale = HD ** -0.5
    c = c_ref[0, 0]
    ccol = _row_to_col(c)
    k = k_ref[...].astype(BF16)
    v = v_ref[...].astype(BF16)
    row = lax.broadcasted_iota(jnp.int32, (tq, tq), 0)
    col = lax.broadcasted_iota(jnp.int32, (tq, tq), 1)
    for i in range(t // tq):
        lo, hi = i * tq, (i + 1) * tq
        q = (q_ref[lo:hi, :] * scale).astype(BF16)
        cq = ccol[lo:hi]
        sd = lax.dot_general(q, k[lo:hi], NT_DIMS, preferred_element_type=F32) + cq - c[:, lo:hi]
        sd = jnp.where(col <= row, sd, NEG_INF)
        m = jnp.max(sd, axis=-1, keepdims=True)
        if i > 0:
            sp = lax.dot_general(q, k[0:lo], NT_DIMS, preferred_element_type=F32) + cq - c[:, 0:lo]
            m = jnp.maximum(m, jnp.max(sp, axis=-1, keepdims=True))
        pd = jnp.exp(sd - m)
        l = jnp.sum(pd, axis=-1, keepdims=True)
        o = jnp.dot(pd.astype(BF16), v[lo:hi], preferred_element_type=F32)
        if i > 0:
            pp = jnp.exp(sp - m)
            l = l + jnp.sum(pp, axis=-1, keepdims=True)
            o = o + jnp.dot(pp.astype(BF16), v[0:lo], preferred_element_type=F32)
        o_ref[lo:hi, :] = (o / l).astype(o_ref.dtype)


def _fox_prompt(z, c_row, nb, t, offs):
    qb, kb, vb = (offs[n] // HD for n in ("fq", "fk", "fv"))
    tq = min(256, t)
    return pl.pallas_call(
        functools.partial(_fox_prompt_body, tq=tq),
        grid=(nb, FOX_H),
        in_specs=[pl.BlockSpec((t, HD), lambda b, h: (b, qb + h)),
                  pl.BlockSpec((t, HD), lambda b, h: (b, kb + h)),
                  pl.BlockSpec((t, HD), lambda b, h: (b, vb + h)),
                  pl.BlockSpec((1, 1, 1, t), lambda b, h: (b, h, 0, 0))],
        out_specs=pl.BlockSpec((t, HD), lambda b, h: (b, h)),
        out_shape=jax.ShapeDtypeStruct((nb * t, FOX_H * HD), BF16),
        compiler_params=_cparams("parallel", "parallel"),
        name="fox_prompt",
    )(z, z, z, c_row.reshape(nb, FOX_H, 1, t))


def _strided_roll_add(x, shift, mask=None):
    y = pltpu.roll(x, shift, 1)
    return x + (y if mask is None else jnp.where(mask, y, 0.0))


def _fox_cache_sums_body(lf_ref, o_ref):
    lf = lf_ref[0]
    n = lf.shape[1]
    lane = lax.broadcasted_iota(jnp.int32, lf.shape, 1)
    incl, tot = lf, lf
    d = FOX_H
    while d < n:
        incl = _strided_roll_add(incl, n - d, lane < n - d)
        tot = _strided_roll_add(tot, d)
        d *= 2
    o_ref[0, :, 0:n] = incl - lf
    o_ref[0, :, n:2 * n] = tot


def _fox_cache_sums(cache_logf):
    depth, n_pool, page, nh = cache_logf.shape
    n = page * nh
    blk = next(t for t in (256, 128, 64, 32, 16, 8, n_pool) if n_pool % t == 0)
    sums = pl.pallas_call(
        _fox_cache_sums_body,
        grid=(depth, n_pool // blk),
        in_specs=[pl.BlockSpec((1, blk, n), lambda l, i: (l, i, 0))],
        out_specs=pl.BlockSpec((1, blk, 2 * n), lambda l, i: (l, i, 0)),
        out_shape=jax.ShapeDtypeStruct((depth, n_pool, 2 * n), F32),
        compiler_params=_cparams("parallel", "parallel"),
        name="fox_cache_sums",
    )(cache_logf.reshape(depth, n_pool, n))
    return sums.reshape(depth, n_pool, 2, n)


def _fox_gates_body(zf_ref, bias_ref, lf_ref, cum_ref, *, n_tok):
    rows = zf_ref.shape[0] * PAD_L
    lf = _log_sigmoid(zf_ref[:, :, 0:LANES].reshape(rows, LANES) + bias_ref[...])
    lf_ref[...] = lf
    pos = lax.broadcasted_iota(jnp.int32, (rows, 1), 0) & (PAD_L - 1)
    cum = jnp.where(pos < n_tok, lf, 0.0)
    d = 1
    while d < PAD_L:
        cum = cum + jnp.where(pos >= d, pltpu.roll(cum, d, 0), 0.0)
        d *= 2
    cum_ref[...] = cum


def _fox_gates(zs, bias, ff_off, n_tok):
    db = zs.shape[0]
    out = jax.ShapeDtypeStruct((db * PAD_L, LANES), F32)
    return pl.pallas_call(
        functools.partial(_fox_gates_body, n_tok=n_tok),
        grid=(1,),
        in_specs=[pl.BlockSpec((db, PAD_L, SEG), lambda i: (0, 0, ff_off // SEG)),
                  pl.BlockSpec((1, LANES), lambda i: (0, 0))],
        out_specs=[pl.BlockSpec((db * PAD_L, LANES), lambda i: (0, 0))] * 2,
        out_shape=[out, out],
        compiler_params=_cparams("arbitrary"),
        name="fox_gates",
    )(zs, bias)


def _fox_sample_body(pt_ref, q_ref, kn_ref, vn_ref, ccol_ref, crow_ref, *rest, n_grp):
    kp, vp, sums = rest[0:n_grp], rest[n_grp:2 * n_grp], rest[2 * n_grp:3 * n_grp]
    o_ref, m_ref, l_ref, acc_ref, carry_ref = rest[3 * n_grp:]
    p = pl.program_id(1)
    q = (q_ref[0] * HD ** -0.5).astype(BF16)

    def same_head(shape):
        r = lax.broadcasted_iota(jnp.int32, shape, 0)
        c = lax.broadcasted_iota(jnp.int32, shape, 1)
        return (r & (FOX_H - 1)) == (c & (FOX_H - 1)), r, c

    @pl.when(p == 0)
    def _():
        s = lax.dot_general(q, kn_ref[0].astype(BF16), NT_DIMS, preferred_element_type=F32)
        s = s + ccol_ref[0] - crow_ref[0]
        ok, r, c = same_head(s.shape)
        s = jnp.where(ok & ((c >> 3) <= (r >> 3)), s, NEG_INF)
        m = jnp.max(s, axis=-1, keepdims=True)
        pn = jnp.exp(s - m)
        m_ref[...] = m
        l_ref[...] = jnp.sum(pn, axis=-1, keepdims=True)
        acc_ref[...] = jnp.dot(pn.astype(BF16), vn_ref[0].astype(BF16), preferred_element_type=F32)
        carry_ref[...] = jnp.zeros_like(carry_ref)

    ccol = ccol_ref[0]
    carry = carry_ref[...]
    scores = [None] * n_grp
    for j in reversed(range(n_grp)):
        k2 = kp[j][0, 0].reshape(-1, HD).astype(BF16)
        s = lax.dot_general(q, k2, NT_DIMS, preferred_element_type=F32)
        s = s + ccol + (sums[j][0, 0, 0:1, :] + carry)
        carry = carry + sums[j][0, 0, 1:2, :]
        scores[j] = jnp.where(same_head(s.shape)[0], s, NEG_INF)
    carry_ref[...] = carry
    m_old = m_ref[...]
    m_new = m_old
    for s in scores:
        m_new = jnp.maximum(m_new, jnp.max(s, axis=-1, keepdims=True))
    alpha = jnp.exp(m_old - m_new)
    l_run = alpha * l_ref[...]
    acc = alpha * acc_ref[...]
    for j in range(n_grp):
        pp = jnp.exp(scores[j] - m_new)
        l_run = l_run + jnp.sum(pp, axis=-1, keepdims=True)
        v2 = vp[j][0, 0].reshape(-1, HD).astype(BF16)
        acc = acc + jnp.dot(pp.astype(BF16), v2, preferred_element_type=F32)
    m_ref[...], l_ref[...], acc_ref[...] = m_new, l_run, acc

    @pl.when(p == pl.num_programs(1) - 1)
    def _():
        o_ref[0] = (acc / l_run).astype(o_ref.dtype)


def _fox_sample(q, kn, vn, cum_col, cum_row, cache_k, cache_v, cache_sums, page_table, layer):
    db, nrow, _ = q.shape
    n_pages = page_table.shape[1]
    page = cache_k.shape[2]
    n_grp = next(g for g in (PAGE_GROUP, 4, 2, 1) if n_pages % g == 0)
    n_flat = page * FOX_H

    def page_map(j):
        return lambda b, p, pt: (layer, pt[b * n_pages + n_pages - (p + 1) * n_grp + j], 0, 0, 0)

    def sums_map(j):
        return lambda b, p, pt: (layer, pt[b * n_pages + n_pages - (p + 1) * n_grp + j], 0, 0)

    seq = lambda r, c: pl.BlockSpec((1, r, c), lambda b, p, pt: (b, 0, 0))
    kv_specs = [pl.BlockSpec((1, 1, page, FOX_H, HD), page_map(j)) for j in range(n_grp)]
    grid_spec = pltpu.PrefetchScalarGridSpec(
        num_scalar_prefetch=1,
        grid=(db, n_pages // n_grp),
        in_specs=[seq(nrow, HD), seq(PAD_L * FOX_H, HD), seq(PAD_L * FOX_H, HD), seq(nrow, 1), seq(1, PAD_L * FOX_H)]
                 + kv_specs + kv_specs
                 + [pl.BlockSpec((1, 1, 2, n_flat), sums_map(j)) for j in range(n_grp)],
        out_specs=seq(nrow, HD),
        scratch_shapes=[pltpu.VMEM((nrow, 1), F32), pltpu.VMEM((nrow, 1), F32), pltpu.VMEM((nrow, HD), F32),
                        pltpu.VMEM((1, n_flat), F32)],
    )
    return pl.pallas_call(
        functools.partial(_fox_sample_body, n_grp=n_grp),
        grid_spec=grid_spec,
        out_shape=jax.ShapeDtypeStruct((db, nrow, HD), BF16),
        compiler_params=_cparams("parallel", "arbitrary"),
        name="fox_sample",
    )(page_table.reshape(-1), q, kn, vn, cum_col, cum_row, *([cache_k] * n_grp), *([cache_v] * n_grp),
      *([cache_sums] * n_grp))


def _hg_chunk(q, k, v, lf, s0, k_scr=None, b_scr=None):
    c = q.shape[0]
    if c > SUB:
        tri = (lax.broadcasted_iota(jnp.int32, (c, c), 1) <= lax.broadcasted_iota(jnp.int32, (c, c), 0)).astype(F32)
        b = jnp.dot(tri, lf, preferred_element_type=F32, precision=lax.Precision.HIGHEST)
    else:
        b = _cumsum_rows(lf)
    o = jnp.dot((q * jnp.exp(b)).astype(BF16), s0.astype(BF16), preferred_element_type=F32)
    vb = v.astype(BF16)

    row = lax.broadcasted_iota(jnp.int32, (c, 1), 0)
    a_mat = None
    half = SUB
    while 2 * half <= c:
        nblk = c // (2 * half)
        shift = (2 * half).bit_length() - 1
        split = jnp.broadcast_to(b.reshape(nblk, 2 * half, HD)[:, half - 1:half, :],
                                 (nblk, 2 * half, HD)).reshape(c, HD)
        right = (row & (2 * half - 1)) >= half
        qt = jnp.where(right, q * jnp.exp(b - split), 0.0).astype(BF16)
        kt = jnp.where(right, 0.0, k * jnp.exp(split - b)).astype(BF16)
        a_l = lax.dot_general(qt, kt, NT_DIMS, preferred_element_type=F32)
        ri = lax.broadcasted_iota(jnp.int32, (c, c), 0) >> shift
        ci = lax.broadcasted_iota(jnp.int32, (c, c), 1) >> shift
        a_l = jnp.where(ri == ci, a_l, 0.0)
        a_mat = a_l if a_mat is None else a_mat + a_l
        half *= 2

    if k_scr is not None:
        k_scr[...] = k
        b_scr[...] = b
        k_row = lambda r: k_scr[r:r + 1, :]
        b_row = lambda r: b_scr[r:r + 1, :]
    else:
        k_row = lambda r: k[r:r + 1, :]
        b_row = lambda r: b[r:r + 1, :]
    sub_row = lax.broadcasted_iota(jnp.int32, (SUB, 1), 0)
    if a_mat is None:
        ob = jnp.zeros((SUB, HD), F32)
        for s in range(SUB):
            a_s = jnp.sum(q * k_row(s) * jnp.exp(b - b_row(s)), axis=-1, keepdims=True)
            ob = ob + jnp.where(sub_row >= s, a_s, 0.0) * v[s:s + 1, :]
        o = o + ob
    else:
        lane = lax.broadcasted_iota(jnp.int32, (SUB, c), 1)
        blocks = []
        for blk in range(c // SUB):
            sl = slice(blk * SUB, (blk + 1) * SUB)
            qb_, bb_ = q[sl], b[sl]
            acc = jnp.zeros((SUB, c), F32)
            for s in range(SUB):
                r = blk * SUB + s
                a_s = jnp.sum(qb_ * k_row(r) * jnp.exp(bb_ - b_row(r)), axis=-1, keepdims=True)
                acc = jnp.where(lane == r, a_s, acc)
            blocks.append(jnp.where(lane <= sub_row + blk * SUB, acc, 0.0))
        a_mat = a_mat + jnp.concatenate(blocks, axis=0)
        o = o + jnp.dot(a_mat.astype(BF16), vb, preferred_element_type=F32)

    b_last = b[c - 1:c, :]
    kt = (k * jnp.exp(b_last - b)).astype(BF16)
    s_new = _row_to_col(jnp.exp(b_last)) * s0 + lax.dot_general(kt, vb, TN_DIMS, preferred_element_type=F32)
    return o, s_new


def _hg_gates(hq, hf, lb):
    sig = _sigmoid(hf)
    lf = jnp.log(lb + (1.0 - lb) * sig)
    k = (1.0 - lb) * _sigmoid(-hf)
    return _silu(hq), k, lf


def _hg_prompt_body(hq_ref, hf_ref, hi_ref, hg_ref, lb_ref, gn_ref, o_ref, s_ref, k_scr, b_scr, *, chunk):
    t = hq_ref.shape[0]
    n_heads = hq_ref.shape[1] // HD
    gn = gn_ref[...]

    def step(ci, states):
        rows = pl.ds(pl.multiple_of(ci * chunk, chunk), chunk)
        new_states = []
        for h in range(n_heads):
            cs = slice(h * HD, (h + 1) * HD)
            q, k, lf = _hg_gates(hq_ref[rows, cs], hf_ref[rows, cs], lb_ref[:, cs])
            o, s = _hg_chunk(q, k, hi_ref[rows, cs], lf, states[h], k_scr.at[h], b_scr.at[h])
            o_ref[rows, cs] = (_rms(o, gn) * _silu(hg_ref[rows, cs])).astype(o_ref.dtype)
            new_states.append(s)
        return tuple(new_states)

    states = lax.fori_loop(0, t // chunk, step, tuple(jnp.zeros((HD, HD), F32) for _ in range(n_heads)))
    for h in range(n_heads):
        s_ref[0, h] = states[h]


def _hg_prompt(z, lb, gn, nb, t, offs, heads_per_step=4):
    hw = heads_per_step * HD
    qb, fb, ib, gb = (offs[n] // hw for n in ("hq", "hf", "hi", "hg"))
    chunk = min(128, t)
    col = lambda base: pl.BlockSpec((t, hw), lambda b, h: (b, base + h))
    return pl.pallas_call(
        functools.partial(_hg_prompt_body, chunk=chunk),
        grid=(nb, HG_H // heads_per_step),
        in_specs=[col(qb), col(fb), col(ib), col(gb),
                  pl.BlockSpec((1, hw), lambda b, h: (0, h)),
                  pl.BlockSpec((1, HD), lambda b, h: (0, 0))],
        out_specs=[pl.BlockSpec((t, hw), lambda b, h: (b, h)),
                   pl.BlockSpec((1, heads_per_step, HD, HD), lambda b, h: (b, h, 0, 0))],
        out_shape=[jax.ShapeDtypeStruct((nb * t, HG_H * HD), BF16),
                   jax.ShapeDtypeStruct((nb, HG_H, HD, HD), F32)],
        scratch_shapes=[pltpu.VMEM((heads_per_step, chunk, HD), F32)] * 2,
        compiler_params=_cparams("parallel", "parallel"),
        name="hgrn2_prompt",
    )(z, z, z, z, lb, gn)


def _hg_sample_body(hq_ref, hf_ref, hi_ref, hg_ref, lb_ref, gn_ref, s0_ref, o_ref, s_ref, *, n_tok):
    valid = lax.broadcasted_iota(jnp.int32, (PAD_L, 1), 0) < n_tok
    gn = gn_ref[...]
    for i in range(hq_ref.shape[0]):
        for h in range(HG_H):
            cs = slice(h * HD, (h + 1) * HD)
            q, k, lf = _hg_gates(hq_ref[i, :, cs], hf_ref[i, :, cs], lb_ref[:, cs])
            k = jnp.where(valid, k, 0.0)
            lf = jnp.where(valid, lf, 0.0)
            o, s = _hg_chunk(q, k, hi_ref[i, :, cs], lf, s0_ref[0, i, h])
            s_ref[i, h] = s
            o_ref[i, :, cs] = (_rms(o, gn) * _silu(hg_ref[i, :, cs])).astype(o_ref.dtype)


def _hg_sample(zs, lb, gn, state, layer, offs, n_tok):
    db = zs.shape[0]
    w = HG_H * HD
    ns = 2 if db % 2 == 0 else 1
    col = lambda name: pl.BlockSpec((ns, PAD_L, w), lambda b: (b, 0, offs[name] // w))
    return pl.pallas_call(
        functools.partial(_hg_sample_body, n_tok=n_tok),
        grid=(db // ns,),
        in_specs=[col("hq"), col("hf"), col("hi"), col("hg"),
                  pl.BlockSpec((1, w), lambda b: (0, 0)),
                  pl.BlockSpec((1, HD), lambda b: (0, 0)),
                  pl.BlockSpec((1, ns, HG_H, HD, HD), lambda b: (layer, b, 0, 0, 0))],
        out_specs=[pl.BlockSpec((ns, PAD_L, w), lambda b: (b, 0, 0)),
                   pl.BlockSpec((ns, HG_H, HD, HD), lambda b: (b, 0, 0, 0))],
        out_shape=[jax.ShapeDtypeStruct((db, PAD_L, w), BF16),
                   jax.ShapeDtypeStruct((db, HG_H, HD, HD), F32)],
        compiler_params=_cparams("parallel"),
        name="hgrn2_sample",
    )(zs, zs, zs, zs, lb, gn, state)


def _lru_gates(xc, wa_ref, ba, wx_ref, bx, lam):
    nblk = xc.shape[1] // HD
    xb = xc.astype(BF16)
    pa = [jnp.dot(xb[:, i * HD:(i + 1) * HD], wa_ref[i], preferred_element_type=F32) for i in range(nblk)]
    px = [jnp.dot(xb[:, i * HD:(i + 1) * HD], wx_ref[i], preferred_element_type=F32) for i in range(nblk)]
    r = _sigmoid(jnp.concatenate(pa, axis=1) + ba)
    ig = _sigmoid(jnp.concatenate(px, axis=1) + bx)
    log_a = -LRU_C * r * _softplus(-lam)
    a = jnp.exp(log_a)
    u = jnp.sqrt(-jnp.tanh(log_a) * (1.0 + a * a)) * (ig * xc)
    return a, u


def _lru_prompt_body(lx_ref, ly_ref, cw_ref, cb_ref, wa_ref, ba_ref, wx_ref, bx_ref, lam_ref,
                     o_ref, h_ref, *, rt):
    t, wt = lx_ref.shape
    cw = cw_ref[...]
    sub_row = lax.broadcasted_iota(jnp.int32, (SUB, 1), 0)
    pos = lax.broadcasted_iota(jnp.int32, (rt, 1), 0) & (SUB - 1)

    def tile(ti, carry):
        h, tail = carry
        rows = pl.ds(pl.multiple_of(ti * rt, rt), rt)
        x = lx_ref[rows, :]
        xc = cb_ref[...] + x * cw[CONV_W - 1:CONV_W, :]
        for d in range(1, CONV_W):
            xr = pltpu.roll(x, d, 0)
            head = jnp.where(sub_row < d, pltpu.roll(tail, d, 0), xr[0:SUB])
            xs = jnp.concatenate([head, xr[SUB:]], axis=0) if rt > SUB else head
            xc = xc + xs * cw[CONV_W - 1 - d:CONV_W - d, :]
        a, u = _lru_gates(xc, wa_ref, ba_ref[...], wx_ref, bx_ref[...], lam_ref[...])
        d = 1
        while d < SUB:
            keep = pos >= d
            u = a * jnp.where(keep, pltpu.roll(u, d, 0), 0.0) + u
            a = a * jnp.where(keep, pltpu.roll(a, d, 0), 1.0)
            d *= 2
        gelu = _gelu_tanh(ly_ref[rows, :])
        for g in range(rt // SUB):
            sl = slice(g * SUB, (g + 1) * SUB)
            hs = a[sl] * h + u[sl]
            h = hs[SUB - 1:SUB, :]
            o_ref[pl.ds(pl.multiple_of(ti * rt, rt) + g * SUB, SUB), :] = (hs * gelu[sl]).astype(o_ref.dtype)
        return h, x[rt - SUB:rt, :]

    h, _ = lax.fori_loop(0, t // rt, tile, (jnp.zeros((1, wt), F32), jnp.zeros((SUB, wt), F32)))
    h_ref[0] = h


def _lru_prompt(z, p, nb, t, offs):
    w = p["lru_conv_b"].shape[1]
    wt = SEG
    nblk = wt // HD
    rt = min(128, t)
    xb, yb = offs["lx"] // wt, offs["ly"] // wt
    vec = pl.BlockSpec((1, wt), lambda b, c: (0, c))
    gate_w = pl.BlockSpec((nblk, HD, HD), lambda b, c: (c, 0, 0))
    return pl.pallas_call(
        functools.partial(_lru_prompt_body, rt=rt),
        grid=(nb, w // wt),
        in_specs=[pl.BlockSpec((t, wt), lambda b, c: (b, xb + c)),
                  pl.BlockSpec((t, wt), lambda b, c: (b, yb + c)),
                  pl.BlockSpec((CONV_W, wt), lambda b, c: (0, c)),
                  vec, gate_w, vec, gate_w, vec, vec],
        out_specs=[pl.BlockSpec((t, wt), lambda b, c: (b, c)),
                   pl.BlockSpec((1, 1, wt), lambda b, c: (b, 0, c))],
        out_shape=[jax.ShapeDtypeStruct((nb * t, w), BF16),
                   jax.ShapeDtypeStruct((nb, 1, w), F32)],
        compiler_params=_cparams("parallel", "parallel"),
        name="lru_prompt",
    )(z, z, p["lru_conv_w"], p["lru_conv_b"], p["lru_w_a"], p["lru_b_a"], p["lru_w_x"], p["lru_b_x"],
      p["lru_lambda"])


def _lru_sample_body(*refs, n_tok):
    lx = refs[0:n_tok]
    ly = refs[n_tok:2 * n_tok]
    buf = refs[2 * n_tok:2 * n_tok + CONV_W - 1]
    h0_ref, cw_ref, cb_ref, wa_ref, ba_ref, wx_ref, bx_ref, lam_ref = refs[2 * n_tok + CONV_W - 1:2 * n_tok + CONV_W + 7]
    outs = refs[2 * n_tok + CONV_W + 7:]
    o_refs, h_ref = outs[0:n_tok], outs[n_tok]
    cw = cw_ref[...]
    xp = [r[0] for r in buf] + [r[...] for r in lx]
    h = h0_ref[0]
    for t in range(n_tok):
        xc = cb_ref[...]
        for j in range(CONV_W):
            xc = xc + xp[t + j] * cw[j:j + 1, :]
        a, u = _lru_gates(xc, wa_ref, ba_ref[...], wx_ref, bx_ref[...], lam_ref[...])
        h = a * h + u
        o_refs[t][...] = (h * _gelu_tanh(ly[t][...])).astype(o_refs[t].dtype)
    h_ref[...] = h


def _lru_sample(zs2, p, state_lru, state_conv, layer, offs, n_tok, nz):
    db = zs2.shape[0]
    w = p["lru_conv_b"].shape[1]
    wt = SEG
    nblk = wt // HD
    sc = state_conv.reshape(state_conv.shape[0], db, (CONV_W - 1) * w)
    tok = lambda name, l: pl.BlockSpec((db, wt), lambda c: (0, (l * nz + offs[name]) // wt + c))
    vec = pl.BlockSpec((1, wt), lambda c: (0, c))
    gate_w = pl.BlockSpec((nblk, HD, HD), lambda c: (c, 0, 0))
    in_specs = ([tok("lx", l) for l in range(n_tok)] + [tok("ly", l) for l in range(n_tok)]
                + [pl.BlockSpec((1, db, wt), lambda c, j=j: (layer, 0, j * (w // wt) + c)) for j in range(CONV_W - 1)]
                + [pl.BlockSpec((1, db, wt), lambda c: (layer, 0, c)),
                   pl.BlockSpec((CONV_W, wt), lambda c: (0, c)),
                   vec, gate_w, vec, gate_w, vec, vec])
    out_spec = pl.BlockSpec((db, wt), lambda c: (0, c))
    outs = pl.pallas_call(
        functools.partial(_lru_sample_body, n_tok=n_tok),
        grid=(w // wt,),
        in_specs=in_specs,
        out_specs=[out_spec] * (n_tok + 1),
        out_shape=[jax.ShapeDtypeStruct((db, w), BF16)] * n_tok + [jax.ShapeDtypeStruct((db, w), F32)],
        compiler_params=_cparams("parallel"),
        name="lru_sample",
    )(*([zs2] * (2 * n_tok) + [sc] * (CONV_W - 1)
        + [state_lru, p["lru_conv_w"], p["lru_conv_b"], p["lru_w_a"], p["lru_b_a"], p["lru_w_x"], p["lru_b_x"],
           p["lru_lambda"]]))
    return jnp.stack(outs[:n_tok], axis=1).reshape(db * n_tok, w), outs[n_tok]


def _xattn_prompt_body(q_ref, k_ref, v_ref, o_ref):
    scale = HD ** -0.5
    for h in range(X_H):
        cs = slice(h * HD, (h + 1) * HD)
        q = (q_ref[:, cs] * scale).astype(BF16)
        s = lax.dot_general(q, k_ref[0, :, cs].astype(BF16), NT_DIMS, preferred_element_type=F32)
        p = jnp.exp(s - jnp.max(s, axis=-1, keepdims=True))
        l = jnp.sum(p, axis=-1, keepdims=True)
        o = jnp.dot(p.astype(BF16), v_ref[0, :, cs].astype(BF16), preferred_element_type=F32) / l
        o_ref[:, cs] = o.astype(o_ref.dtype)


def _xattn_prompt(q, mk, mv, nb, t):
    w = X_H * HD
    m = mk.shape[1]
    tq = min(512, t)
    return pl.pallas_call(
        _xattn_prompt_body,
        grid=(nb, t // tq),
        in_specs=[pl.BlockSpec((tq, w), lambda b, i: (b * (t // tq) + i, 0)),
                  pl.BlockSpec((1, m, w), lambda b, i: (b, 0, 0)),
                  pl.BlockSpec((1, m, w), lambda b, i: (b, 0, 0))],
        out_specs=pl.BlockSpec((tq, w), lambda b, i: (b * (t // tq) + i, 0)),
        out_shape=jax.ShapeDtypeStruct((nb * t, w), BF16),
        compiler_params=_cparams("parallel", "arbitrary"),
        name="xattn_prompt",
    )(q, mk, mv)


def _xattn_sample_body(q_ref, k_ref, v_ref, o_ref, *, n_seq):
    scale = HD ** -0.5
    for i in range(n_seq):
        q = (q_ref[i] * scale).astype(BF16)
        s = lax.dot_general(q, k_ref[0, i].astype(BF16), NT_DIMS, preferred_element_type=F32)
        r = lax.broadcasted_iota(jnp.int32, s.shape, 0)
        c = lax.broadcasted_iota(jnp.int32, s.shape, 1)
        s = jnp.where((r & (X_H - 1)) == (c & (X_H - 1)), s, NEG_INF)
        p = jnp.exp(s - jnp.max(s, axis=-1, keepdims=True))
        l = jnp.sum(p, axis=-1, keepdims=True)
        o = jnp.dot(p.astype(BF16), v_ref[0, i].astype(BF16), preferred_element_type=F32) / l
        o_ref[i] = o.astype(o_ref.dtype)


def _xattn_sample(q, cache_mk, cache_mv, layer):
    db, nrow, _ = q.shape
    mx = cache_mk.shape[2]
    n_seq = min(8, db)
    kv = pl.BlockSpec((1, n_seq, mx, HD), lambda b: (layer, b, 0, 0))
    return pl.pallas_call(
        functools.partial(_xattn_sample_body, n_seq=n_seq),
        grid=(db // n_seq,),
        in_specs=[pl.BlockSpec((n_seq, nrow, HD), lambda b: (b, 0, 0)), kv, kv],
        out_specs=pl.BlockSpec((n_seq, nrow, HD), lambda b: (b, 0, 0)),
        out_shape=jax.ShapeDtypeStruct((db, nrow, HD), BF16),
        compiler_params=_cparams("parallel"),
        name="xattn_sample",
    )(q, cache_mk, cache_mv)


def _segment_offsets(d):
    fox_w, hg_w, lru_w = FOX_H * HD, HG_H * HD, HG_H * HD
    names = [("fq", fox_w), ("fk", fox_w), ("fv", fox_w), ("hq", hg_w), ("hf", hg_w), ("hi", hg_w), ("hg", hg_w),
             ("lx", lru_w), ("ly", lru_w), ("ga", d), ("gb", d), ("gc", d), ("ff", FF_SEG)]
    offs, o = {}, 0
    for n, wdt in names:
        offs[n] = o
        o += wdt
    return offs, o


def _rearrange_body(main_ref, extra_ref, o_ref, *, n_plain, n_shift):
    c = pl.program_id(2)
    tr, tw = o_ref.shape

    @pl.when(c < n_plain)
    def _():
        o_ref[...] = main_ref[...].T.astype(o_ref.dtype)

    @pl.when((c >= n_plain) & (c < n_shift))
    def _():
        y = jnp.concatenate([main_ref[...], extra_ref[...]], axis=0)
        o_ref[...] = y[FOX_H:FOX_H + tw].T.astype(o_ref.dtype)

    @pl.when(c == n_shift)
    def _():
        y = jnp.concatenate([extra_ref[...], jnp.zeros((tw - FOX_H, tr), F32)], axis=0)
        o_ref[...] = y.T.astype(o_ref.dtype)

    @pl.when(c > n_shift)
    def _():
        o_ref[...] = jnp.zeros_like(o_ref)


def _rearrange_w_in(w_in, nz):
    depth, d, n_in = w_in.shape
    assert FOX_H == SUB
    fw3 = 3 * FOX_H * HD
    off_ff = n_in - FOX_H
    tw = next(t for t in (1024, 512, 256, 128) if fw3 % t == 0 and off_ff % t == 0 and nz % t == 0)
    tr = _row_tile(d, cap=512)
    n_plain, n_shift = fw3 // tw, off_ff // tw

    def main_map(l, r, c):
        return (l, jnp.minimum(c, n_shift - 1), r)

    def extra_map(l, r, c):
        shifted = (c >= n_plain) & (c < n_shift)
        return (l, jnp.where(shifted, (c + 1) * (tw // SUB), fw3 // SUB), r)

    w_t = jnp.swapaxes(w_in, 1, 2)
    return pl.pallas_call(
        functools.partial(_rearrange_body, n_plain=n_plain, n_shift=n_shift),
        grid=(depth, d // tr, nz // tw),
        in_specs=[pl.BlockSpec((None, tw, tr), main_map), pl.BlockSpec((None, SUB, tr), extra_map)],
        out_specs=pl.BlockSpec((None, tr, tw), lambda l, r, c: (l, r, c)),
        out_shape=jax.ShapeDtypeStruct((depth, d, nz), BF16),
        compiler_params=_cparams("parallel", "parallel", "arbitrary"),
        name="rearrange_w_in",
    )(w_t, w_t)


def _pad_tokens(x, db, n_tok):
    return jnp.pad(x.reshape(db, n_tok, x.shape[1]), ((0, 0), (0, PAD_L - n_tok), (0, 0)))


def kernel(x_prompt, x_sample, mem_prompt, cache_k, cache_v, cache_logf, state_hgrn, state_lru, state_conv, cache_mem_k, cache_mem_v, page_table, ffn1_norm, ffn1_w_gate, ffn1_w_up, ffn1_w_down, mix_norm, w_in, fox_b_f, fox_q_norm, fox_k_norm, hg_lb_param, hg_out_norm, lru_conv_w, lru_conv_b, lru_w_a, lru_b_a, lru_w_x, lru_b_x, lru_lambda, w_br_fox, w_br_hg, w_br_lru, w_out, xa_norm, xa_mem_norm, xa_w_q, xa_w_kv, xa_q_norm, xa_k_norm, xa_w_o, ffn2_norm, ffn2_w_gate, ffn2_w_up, ffn2_w_down):
    nb, t, d = x_prompt.shape
    db, n_tok, _ = x_sample.shape
    depth = w_in.shape[0]
    n_mem = mem_prompt.shape[1]
    np_rows = nb * t
    fw, xw = FOX_H * HD, X_H * HD
    lw = lru_conv_b.shape[1]
    assert n_tok <= PAD_L and n_tok >= CONV_W - 1 and t >= CONV_W - 1
    offs, nz = _segment_offsets(d)

    sm = jax.nn.softmax(hg_lb_param.astype(F32), axis=0)
    hg_lb = jnp.cumsum(sm, axis=0) - sm[0]

    x = jnp.concatenate([x_prompt.reshape(np_rows, d), x_sample.reshape(db * n_tok, d)], axis=0)
    mem = mem_prompt.reshape(nb * n_mem, d)
    cmk = cache_mem_k.reshape(depth, db, n_mem * X_H, HD)
    cmv = cache_mem_v.reshape(depth, db, n_mem * X_H, HD)

    cache_sums = _fox_cache_sums(cache_logf)
    w_in_r = _rearrange_w_in(w_in, nz)
    w16 = {name: w.astype(BF16) for name, w in dict(
        ffn1_g=ffn1_w_gate, ffn1_u=ffn1_w_up, ffn1_d=ffn1_w_down, ffn2_g=ffn2_w_gate, ffn2_u=ffn2_w_up,
        ffn2_d=ffn2_w_down, br_fox=w_br_fox, br_hg=w_br_hg, br_lru=w_br_lru, out=w_out,
        xa_q=xa_w_q, xa_kv=xa_w_kv, xa_o=xa_w_o).items()}

    outs = {n: [] for n in ("pk", "pv", "plf", "ps", "ph", "pc", "pmk", "pmv", "sk", "sv", "slf", "ss", "sh", "sc")}
    for l in range(depth):
        bf = lambda a: a[l].astype(BF16)
        row = lambda a: a[l].reshape(1, -1).astype(F32)
        lru_p = dict(lru_conv_w=lru_conv_w[l], lru_conv_b=row(lru_conv_b), lru_w_a=bf(lru_w_a), lru_b_a=row(lru_b_a),
                     lru_w_x=bf(lru_w_x), lru_b_x=row(lru_b_x), lru_lambda=row(lru_lambda))
        fox_bias = jnp.pad(row(fox_b_f), ((0, 0), (0, LANES - FOX_H)))
        head_gain = jnp.concatenate([jnp.tile(row(fox_q_norm), (1, FOX_H)), jnp.tile(row(fox_k_norm), (1, FOX_H)),
                                     jnp.ones((1, nz - 2 * fw), F32)], axis=1)

        x = _ffn(x, ffn1_norm[l], w16["ffn1_g"], w16["ffn1_u"], w16["ffn1_d"], l)
        z = _norm_matmul(x, mix_norm[l], w_in_r, l, head_gain, n_norm_cols=2 * fw)
        zs = _pad_tokens(z[np_rows:], db, n_tok)

        plf, c_row = _fox_prep(z, fox_bias, nb, t, offs["ff"])
        of_p = _fox_prompt(z, c_row, nb, t, offs)
        slf, cum = _fox_gates(zs, fox_bias, offs["ff"], n_tok)
        slf = slf.reshape(db, PAD_L, LANES)
        cum = cum.reshape(db, PAD_L, LANES)[:, :, :FOX_H]
        heads = lambda name, rows: zs[:, :rows, offs[name]:offs[name] + fw].reshape(db, rows * FOX_H, HD)
        of_s = _fox_sample(heads("fq", n_tok), heads("fk", PAD_L), heads("fv", PAD_L),
                           cum[:, :n_tok].reshape(db, n_tok * FOX_H, 1), cum.reshape(db, 1, PAD_L * FOX_H),
                           cache_k, cache_v, cache_sums, page_table, l).reshape(db * n_tok, fw)
        oh_p, ps = _hg_prompt(z, row(hg_lb), row(hg_out_norm), nb, t, offs)
        oh_s, ss = _hg_sample(zs, row(hg_lb), row(hg_out_norm), state_hgrn, l, offs, n_tok)
        ol_p, ph = _lru_prompt(z, lru_p, nb, t, offs)
        ol_s, sh = _lru_sample(zs.reshape(db, PAD_L * nz), lru_p, state_lru, state_conv, l, offs, n_tok, nz)

        unpad = lambda a: a[:, :n_tok].reshape(db * n_tok, a.shape[2])
        o_fox = jnp.concatenate([of_p, of_s], axis=0)
        o_hg = jnp.concatenate([oh_p, unpad(oh_s)], axis=0)
        o_lru = jnp.concatenate([ol_p, ol_s], axis=0)
        x = _merge(o_fox, o_hg, o_lru, w16["br_fox"], w16["br_hg"], w16["br_lru"], z, offs["ga"], w16["out"], x, l)

        xq = _norm_matmul(x, xa_norm[l], w16["xa_q"], l, jnp.tile(row(xa_q_norm), (1, X_H)), n_norm_cols=xw)
        kv_gain = jnp.concatenate([jnp.tile(row(xa_k_norm), (1, X_H)), jnp.ones((1, xw), F32)], axis=1)
        mkv = _norm_matmul(mem, xa_mem_norm[l], w16["xa_kv"], l, kv_gain, n_norm_cols=xw)
        mk, mv = mkv[:, :xw].reshape(nb, n_mem, xw), mkv[:, xw:].reshape(nb, n_mem, xw)
        ox_p = _xattn_prompt(xq[:np_rows], mk, mv, nb, t)
        ox_s = _xattn_sample(xq[np_rows:].reshape(db, n_tok * X_H, HD), cmk, cmv, l).reshape(db * n_tok, xw)
        x = _matmul_residual(jnp.concatenate([ox_p, ox_s], axis=0), w16["xa_o"], x, l)

        x = _ffn(x, ffn2_norm[l], w16["ffn2_g"], w16["ffn2_u"], w16["ffn2_d"], l)

        zp, zsm = z[:np_rows], z[np_rows:]
        outs["pk"].append(zp[:, offs["fk"]:offs["fk"] + fw])
        outs["pv"].append(zp[:, offs["fv"]:offs["fv"] + fw])
        outs["plf"].append(plf)
        outs["ps"].append(ps)
        outs["ph"].append(ph.reshape(nb, lw))
        outs["pc"].append(zp[:, offs["lx"]:offs["lx"] + lw].reshape(nb, t, lw)[:, t - (CONV_W - 1):])
        outs["pmk"].append(mk.reshape(nb, n_mem, X_H, HD))
        outs["pmv"].append(mv.reshape(nb, n_mem, X_H, HD))
        outs["sk"].append(zsm[:, offs["fk"]:offs["fk"] + fw].reshape(db, n_tok, FOX_H, HD))
        outs["sv"].append(zsm[:, offs["fv"]:offs["fv"] + fw].reshape(db, n_tok, FOX_H, HD))
        outs["slf"].append(slf[:, :n_tok, :FOX_H])
        outs["ss"].append(ss)
        outs["sh"].append(sh)
        outs["sc"].append(zsm[:, offs["lx"]:offs["lx"] + lw].reshape(db, n_tok, lw)[:, n_tok - (CONV_W - 1):])

    st = lambda n: jnp.stack(outs[n])
    return (x[:np_rows].reshape(nb, t, d), x[np_rows:].reshape(db, n_tok, d),
            st("pk").reshape(depth, nb, t, FOX_H, HD), st("pv").reshape(depth, nb, t, FOX_H, HD),
            st("plf"), st("ps"), st("ph"), st("pc"), st("pmk"), st("pmv"),
            st("sk"), st("sv"), st("slf"), st("ss"), st("sh"), st("sc"))
```

```python
import functools

import jax
import jax.numpy as jnp
from jax import lax
from jax.experimental import pallas as pl
from jax.experimental.pallas import tpu as pltpu

F32 = jnp.float32
BF16 = jnp.bfloat16
EPS = 1e-6
NEG_INF = -1e30
HD = 128
FOX_H = 8
HG_H = 8
X_H = 4
LRU_C = 8.0
CONV_W = 4
SUB = 8
LANES = 128
SEG = 512
FF_SEG = 1024
VMEM_LIMIT = 56 * 1024 * 1024
ROW_TILES = (768, 512, 256, 128, 64, 32, 16, 8)
PAD_L = 8
PAGE_GROUP = 16

NT_DIMS = (((1,), (1,)), ((), ()))
TN_DIMS = (((0,), (0,)), ((), ()))


def _cparams(*sem):
    return pltpu.CompilerParams(dimension_semantics=sem, vmem_limit_bytes=VMEM_LIMIT)


def _row_tile(n, cap=ROW_TILES[0]):
    return next(t for t in ROW_TILES if t <= cap and n % t == 0)


def _col_tile(n, cands, also_divides=0):
    return next(t for t in cands if n % t == 0 and also_divides % t == 0)


def _rms(x, g):
    return x * lax.rsqrt(jnp.mean(x * x, axis=-1, keepdims=True) + EPS) * g


def _sigmoid(x):
    return jax.nn.sigmoid(x)


def _silu(x):
    return x * jax.nn.sigmoid(x)


def _log_sigmoid(x):
    return jnp.minimum(x, 0.0) - jnp.log1p(jnp.exp(-jnp.abs(x)))


def _softplus(x):
    return jnp.maximum(x, 0.0) + jnp.log1p(jnp.exp(-jnp.abs(x)))


def _gelu_tanh(x):
    return 0.5 * x * (1.0 + jnp.tanh(0.7978845608028654 * (x + 0.044715 * (x * x * x))))


def _cumsum_lanes(x, reverse=False):
    lane = lax.broadcasted_iota(jnp.int32, x.shape, 1)
    d = 1
    while d < LANES:
        if reverse:
            x = x + jnp.where(lane < LANES - d, pltpu.roll(x, LANES - d, 1), 0.0)
        else:
            x = x + jnp.where(lane >= d, pltpu.roll(x, d, 1), 0.0)
        d *= 2
    return x


def _cumsum_rows(x):
    c = x.shape[0]
    row = lax.broadcasted_iota(jnp.int32, (c, 1), 0)
    d = 1
    while d < c:
        x = x + jnp.where(row >= d, pltpu.roll(x, d, 0), 0.0)
        d *= 2
    return x


def _row_to_col(r):
    return jnp.broadcast_to(r, (SUB, r.shape[1])).T[:, 0:1]


def _ffn_body(x_ref, g_ref, wg_ref, wu_ref, wd_ref, o_ref, h_ref, acc_ref):
    j = pl.program_id(1)

    @pl.when(j == 0)
    def _():
        h_ref[...] = _rms(x_ref[...], g_ref[...]).astype(BF16)
        acc_ref[...] = jnp.zeros_like(acc_ref)

    h = h_ref[...]
    gt = jnp.dot(h, wg_ref[...], preferred_element_type=F32)
    ut = jnp.dot(h, wu_ref[...], preferred_element_type=F32)
    a = (_silu(gt) * ut).astype(BF16)
    acc_ref[...] += jnp.dot(a, wd_ref[...], preferred_element_type=F32)

    @pl.when(j == pl.num_programs(1) - 1)
    def _():
        o_ref[...] = x_ref[...] + 0.5 * acc_ref[...]


def _ffn(x, g, wg, wu, wd, layer, tf=512):
    n, d = x.shape
    f = wg.shape[2]
    tm = _row_tile(n)
    tf = min(tf, f)
    return pl.pallas_call(
        _ffn_body,
        grid=(n // tm, f // tf),
        in_specs=[
            pl.BlockSpec((tm, d), lambda i, j: (i, 0)),
            pl.BlockSpec((1, d), lambda i, j: (0, 0)),
            pl.BlockSpec((None, d, tf), lambda i, j: (layer, 0, j)),
            pl.BlockSpec((None, d, tf), lambda i, j: (layer, 0, j)),
            pl.BlockSpec((None, tf, d), lambda i, j: (layer, j, 0)),
        ],
        out_specs=pl.BlockSpec((tm, d), lambda i, j: (i, 0)),
        out_shape=jax.ShapeDtypeStruct((n, d), F32),
        scratch_shapes=[pltpu.VMEM((tm, d), BF16), pltpu.VMEM((tm, d), F32)],
        compiler_params=_cparams("parallel", "arbitrary"),
        name="ffn",
    )(x, g.reshape(1, d), wg, wu, wd)


def _nmm_body(x_ref, g_ref, w_ref, gn_ref, o_ref, h_ref, *, n_norm_tiles):
    j = pl.program_id(1)

    @pl.when(j == 0)
    def _():
        h_ref[...] = _rms(x_ref[...], g_ref[...]).astype(BF16)

    y = jnp.dot(h_ref[...], w_ref[...], preferred_element_type=F32)
    if n_norm_tiles == 0:
        o_ref[...] = y
    else:
        @pl.when(j < n_norm_tiles)
        def _():
            tn = y.shape[1]
            parts = [_rms(y[:, c * HD:(c + 1) * HD], gn_ref[:, c * HD:(c + 1) * HD]) for c in range(tn // HD)]
            o_ref[...] = jnp.concatenate(parts, axis=1)

        @pl.when(j >= n_norm_tiles)
        def _():
            o_ref[...] = y


def _norm_matmul(x, g, w, layer, head_gain=None, n_norm_cols=0):
    n, d = x.shape
    nout = w.shape[2]
    tm = _row_tile(n)
    tn = _col_tile(nout, (2048, 1024, 512, 256, 128), n_norm_cols)
    if head_gain is None:
        head_gain = jnp.ones((1, nout), F32)
    return pl.pallas_call(
        functools.partial(_nmm_body, n_norm_tiles=n_norm_cols // tn),
        grid=(n // tm, nout // tn),
        in_specs=[
            pl.BlockSpec((tm, d), lambda i, j: (i, 0)),
            pl.BlockSpec((1, d), lambda i, j: (0, 0)),
            pl.BlockSpec((None, d, tn), lambda i, j: (layer, 0, j)),
            pl.BlockSpec((1, tn), lambda i, j: (0, j)),
        ],
        out_specs=pl.BlockSpec((tm, tn), lambda i, j: (i, j)),
        out_shape=jax.ShapeDtypeStruct((n, nout), F32),
        scratch_shapes=[pltpu.VMEM((tm, d), BF16)],
        compiler_params=_cparams("parallel", "arbitrary"),
        name="norm_matmul",
    )(x, g.reshape(1, d), w, head_gain)


def _mmres_body(a_ref, w_ref, r_ref, o_ref):
    o_ref[...] = r_ref[...] + jnp.dot(a_ref[...], w_ref[...], preferred_element_type=F32)


def _matmul_residual(a, w, res, layer):
    n, k = a.shape
    nout = w.shape[2]
    tm = _row_tile(n)
    tn = _col_tile(nout, (2048, 1024, 512, 256, 128))
    return pl.pallas_call(
        _mmres_body,
        grid=(n // tm, nout // tn),
        in_specs=[
            pl.BlockSpec((tm, k), lambda i, j: (i, 0)),
            pl.BlockSpec((None, k, tn), lambda i, j: (layer, 0, j)),
            pl.BlockSpec((tm, tn), lambda i, j: (i, j)),
        ],
        out_specs=pl.BlockSpec((tm, tn), lambda i, j: (i, j)),
        out_shape=jax.ShapeDtypeStruct((n, nout), F32),
        compiler_params=_cparams("parallel", "arbitrary"),
        name="matmul_residual",
    )(a, w, res)


def _merge_body(of_ref, oh_ref, ol_ref, wf_ref, wh_ref, wl_ref, ga_ref, gb_ref, gc_ref, o_ref):
    y = _sigmoid(ga_ref[...]) * jnp.dot(of_ref[...], wf_ref[...], preferred_element_type=F32)
    y = y + _sigmoid(gb_ref[...]) * jnp.dot(oh_ref[...], wh_ref[...], preferred_element_type=F32)
    y = y + _sigmoid(gc_ref[...]) * jnp.dot(ol_ref[...], wl_ref[...], preferred_element_type=F32)
    o_ref[...] = y.astype(o_ref.dtype)


def _merge(o_fox, o_hg, o_lru, wf, wh, wl, z, gate_off, layer, tn=1024):
    n, k = o_fox.shape
    d = wf.shape[2]
    tm = _row_tile(n)
    tn = min(tn, d)
    gblk = gate_off // tn
    dblk = d // tn
    act = pl.BlockSpec((tm, k), lambda i, j: (i, 0))
    wsp = pl.BlockSpec((None, k, tn), lambda i, j: (layer, 0, j))
    return pl.pallas_call(
        _merge_body,
        grid=(n // tm, dblk),
        in_specs=[act, act, act, wsp, wsp, wsp,
                  pl.BlockSpec((tm, tn), lambda i, j: (i, gblk + j)),
                  pl.BlockSpec((tm, tn), lambda i, j: (i, gblk + dblk + j)),
                  pl.BlockSpec((tm, tn), lambda i, j: (i, gblk + 2 * dblk + j))],
        out_specs=pl.BlockSpec((tm, tn), lambda i, j: (i, j)),
        out_shape=jax.ShapeDtypeStruct((n, d), BF16),
        compiler_params=_cparams("parallel", "arbitrary"),
        name="merge",
    )(o_fox, o_hg, o_lru, wf, wh, wl, z, z, z)


def _fox_prep_body(zf_ref, bias_ref, lf_ref, c_ref):
    t = zf_ref.shape[0]
    lf = _log_sigmoid(zf_ref[:, 0:LANES] + bias_ref[...])
    lf_ref[0] = lf[:, 0:FOX_H]
    x = lf.T[0:FOX_H, :]
    carry = jnp.zeros((FOX_H, 1), F32)
    for blk in range(t // LANES):
        cs = _cumsum_lanes(x[:, blk * LANES:(blk + 1) * LANES]) + carry
        c_ref[0, :, blk * LANES:(blk + 1) * LANES] = cs
        carry = cs[:, LANES - 1:LANES]


def _fox_prep(z, bias, nb, t, ff_off):
    return pl.pallas_call(
        _fox_prep_body,
        grid=(nb,),
        in_specs=[pl.BlockSpec((t, SEG), lambda b: (b, ff_off // SEG)),
                  pl.BlockSpec((1, LANES), lambda b: (0, 0))],
        out_specs=[pl.BlockSpec((1, t, FOX_H), lambda b: (b, 0, 0)),
                   pl.BlockSpec((1, FOX_H, t), lambda b: (b, 0, 0))],
        out_shape=[jax.ShapeDtypeStruct((nb, t, FOX_H), F32),
                   jax.ShapeDtypeStruct((nb, FOX_H, t), F32)],
        compiler_params=_cparams("parallel"),
        name="fox_prep",
    )(z, bias)


def _fox_prompt_body(q_ref, k_ref, v_ref, c_ref, o_ref, *, tq):
    t = q_ref.shape[0]
    scale = HD ** -0.5
    c = c_ref[0, 0]
    ccol = _row_to_col(c)
    k = k_ref[...].astype(BF16)
    v = v_ref[...].astype(BF16)
    row = lax.broadcasted_iota(jnp.int32, (tq, tq), 0)
    col = lax.broadcasted_iota(jnp.int32, (tq, tq), 1)
    for i in range(t // tq):
        lo, hi = i * tq, (i + 1) * tq
        q = (q_ref[lo:hi, :] * scale).astype(BF16)
        cq = ccol[lo:hi]
        sd = lax.dot_general(q, k[lo:hi], NT_DIMS, preferred_element_type=F32) + cq - c[:, lo:hi]
        sd = jnp.where(col <= row, sd, NEG_INF)
        m = jnp.max(sd, axis=-1, keepdims=True)
        if i > 0:
            sp = lax.dot_general(q, k[0:lo], NT_DIMS, preferred_element_type=F32) + cq - c[:, 0:lo]
            m = jnp.maximum(m, jnp.max(sp, axis=-1, keepdims=True))
        pd = jnp.exp(sd - m)
        l = jnp.sum(pd, axis=-1, keepdims=True)
        o = jnp.dot(pd.astype(BF16), v[lo:hi], preferred_element_type=F32)
        if i > 0:
            pp = jnp.exp(sp - m)
            l = l + jnp.sum(pp, axis=-1, keepdims=True)
            o = o + jnp.dot(pp.astype(BF16), v[0:lo], preferred_element_type=F32)
        o_ref[lo:hi, :] = (o / l).astype(o_ref.dtype)


def _fox_prompt(z, c_row, nb, t, offs):
    qb, kb, vb = (offs[n] // HD for n in ("fq", "fk", "fv"))
    tq = min(256, t)
    return pl.pallas_call(
        functools.partial(_fox_prompt_body, tq=tq),
        grid=(nb, FOX_H),
        in_specs=[pl.BlockSpec((t, HD), lambda b, h: (b, qb + h)),
                  pl.BlockSpec((t, HD), lambda b, h: (b, kb + h)),
                  pl.BlockSpec((t, HD), lambda b, h: (b, vb + h)),
                  pl.BlockSpec((1, 1, 1, t), lambda b, h: (b, h, 0, 0))],
        out_specs=pl.BlockSpec((t, HD), lambda b, h: (b, h)),
        out_shape=jax.ShapeDtypeStruct((nb * t, FOX_H * HD), BF16),
        compiler_params=_cparams("parallel", "parallel"),
        name="fox_prompt",
    )(z, z, z, c_row.reshape(nb, FOX_H, 1, t))


def _strided_roll_add(x, shift, mask=None):
    y = pltpu.roll(x, shift, 1)
    return x + (y if mask is None else jnp.where(mask, y, 0.0))


def _fox_cache_sums_body(lf_ref, o_ref):
    lf = lf_ref[0]
    n = lf.shape[1]
    lane = lax.broadcasted_iota(jnp.int32, lf.shape, 1)
    incl, tot = lf, lf
    d = FOX_H
    while d < n:
        incl = _strided_roll_add(incl, n - d, lane < n - d)
        tot = _strided_roll_add(tot, d)
        d *= 2
    o_ref[0, :, 0:n] = incl - lf
    o_ref[0, :, n:2 * n] = tot


def _fox_cache_sums(cache_logf):
    depth, n_pool, page, nh = cache_logf.shape
    n = page * nh
    blk = next(t for t in (256, 128, 64, 32, 16, 8, n_pool) if n_pool % t == 0)
    sums = pl.pallas_call(
        _fox_cache_sums_body,
        grid=(depth, n_pool // blk),
        in_specs=[pl.BlockSpec((1, blk, n), lambda l, i: (l, i, 0))],
        out_specs=pl.BlockSpec((1, blk, 2 * n), lambda l, i: (l, i, 0)),
        out_shape=jax.ShapeDtypeStruct((depth, n_pool, 2 * n), F32),
        compiler_params=_cparams("parallel", "parallel"),
        name="fox_cache_sums",
    )(cache_logf.reshape(depth, n_pool, n))
    return sums.reshape(depth, n_pool, 2, n)


def _fox_gates_body(zf_ref, bias_ref, lf_ref, cum_ref, *, n_tok):
    rows = zf_ref.shape[0] * PAD_L
    lf = _log_sigmoid(zf_ref[:, :, 0:LANES].reshape(rows, LANES) + bias_ref[...])
    lf_ref[...] = lf
    pos = lax.broadcasted_iota(jnp.int32, (rows, 1), 0) & (PAD_L - 1)
    cum = jnp.where(pos < n_tok, lf, 0.0)
    d = 1
    while d < PAD_L:
        cum = cum + jnp.where(pos >= d, pltpu.roll(cum, d, 0), 0.0)
        d *= 2
    cum_ref[...] = cum


def _fox_gates(zs, bias, ff_off, n_tok):
    db = zs.shape[0]
    out = jax.ShapeDtypeStruct((db * PAD_L, LANES), F32)
    return pl.pallas_call(
        functools.partial(_fox_gates_body, n_tok=n_tok),
        grid=(1,),
        in_specs=[pl.BlockSpec((db, PAD_L, SEG), lambda i: (0, 0, ff_off // SEG)),
                  pl.BlockSpec((1, LANES), lambda i: (0, 0))],
        out_specs=[pl.BlockSpec((db * PAD_L, LANES), lambda i: (0, 0))] * 2,
        out_shape=[out, out],
        compiler_params=_cparams("arbitrary"),
        name="fox_gates",
    )(zs, bias)


def _fox_sample_body(pt_ref, q_ref, kn_ref, vn_ref, ccol_ref, crow_ref, *rest, n_grp):
    kp, vp, sums = rest[0:n_grp], rest[n_grp:2 * n_grp], rest[2 * n_grp:3 * n_grp]
    o_ref, m_ref, l_ref, acc_ref, carry_ref = rest[3 * n_grp:]
    p = pl.program_id(1)
    q = (q_ref[0] * HD ** -0.5).astype(BF16)

    def same_head(shape):
        r = lax.broadcasted_iota(jnp.int32, shape, 0)
        c = lax.broadcasted_iota(jnp.int32, shape, 1)
        return (r & (FOX_H - 1)) == (c & (FOX_H - 1)), r, c

    @pl.when(p == 0)
    def _():
        s = lax.dot_general(q, kn_ref[0].astype(BF16), NT_DIMS, preferred_element_type=F32)
        s = s + ccol_ref[0] - crow_ref[0]
        ok, r, c = same_head(s.shape)
        s = jnp.where(ok & ((c >> 3) <= (r >> 3)), s, NEG_INF)
        m = jnp.max(s, axis=-1, keepdims=True)
        pn = jnp.exp(s - m)
        m_ref[...] = m
        l_ref[...] = jnp.sum(pn, axis=-1, keepdims=True)
        acc_ref[...] = jnp.dot(pn.astype(BF16), vn_ref[0].astype(BF16), preferred_element_type=F32)
        carry_ref[...] = jnp.zeros_like(carry_ref)

    ccol = ccol_ref[0]
    carry = carry_ref[...]
    scores = [None] * n_grp
    for j in reversed(range(n_grp)):
        k2 = kp[j][0, 0].reshape(-1, HD).astype(BF16)
        s = lax.dot_general(q, k2, NT_DIMS, preferred_element_type=F32)
        s = s + ccol + (sums[j][0, 0, 0:1, :] + carry)
        carry = carry + sums[j][0, 0, 1:2, :]
        scores[j] = jnp.where(same_head(s.shape)[0], s, NEG_INF)
    carry_ref[...] = carry
    m_old = m_ref[...]
    m_new = m_old
    for s in scores:
        m_new = jnp.maximum(m_new, jnp.max(s, axis=-1, keepdims=True))
    alpha = jnp.exp(m_old - m_new)
    l_run = alpha * l_ref[...]
    acc = alpha * acc_ref[...]
    for j in range(n_grp):
        pp = jnp.exp(scores[j] - m_new)
        l_run = l_run + jnp.sum(pp, axis=-1, keepdims=True)
        v2 = vp[j][0, 0].reshape(-1, HD).astype(BF16)
        acc = acc + jnp.dot(pp.astype(BF16), v2, preferred_element_type=F32)
    m_ref[...], l_ref[...], acc_ref[...] = m_new, l_run, acc

    @pl.when(p == pl.num_programs(1) - 1)
    def _():
        o_ref[0] = (acc / l_run).astype(o_ref.dtype)


def _fox_sample(q, kn, vn, cum_col, cum_row, cache_k, cache_v, cache_sums, page_table, layer):
    db, nrow, _ = q.shape
    n_pages = page_table.shape[1]
    page = cache_k.shape[2]
    n_grp = next(g for g in (PAGE_GROUP, 4, 2, 1) if n_pages % g == 0)
    n_flat = page * FOX_H

    def page_map(j):
        return lambda b, p, pt: (layer, pt[b * n_pages + n_pages - (p + 1) * n_grp + j], 0, 0, 0)

    def sums_map(j):
        return lambda b, p, pt: (layer, pt[b * n_pages + n_pages - (p + 1) * n_grp + j], 0, 0)

    seq = lambda r, c: pl.BlockSpec((1, r, c), lambda b, p, pt: (b, 0, 0))
    kv_specs = [pl.BlockSpec((1, 1, page, FOX_H, HD), page_map(j)) for j in range(n_grp)]
    grid_spec = pltpu.PrefetchScalarGridSpec(
        num_scalar_prefetch=1,
        grid=(db, n_pages // n_grp),
        in_specs=[seq(nrow, HD), seq(PAD_L * FOX_H, HD), seq(PAD_L * FOX_H, HD), seq(nrow, 1), seq(1, PAD_L * FOX_H)]
                 + kv_specs + kv_specs
                 + [pl.BlockSpec((1, 1, 2, n_flat), sums_map(j)) for j in range(n_grp)],
        out_specs=seq(nrow, HD),
        scratch_shapes=[pltpu.VMEM((nrow, 1), F32), pltpu.VMEM((nrow, 1), F32), pltpu.VMEM((nrow, HD), F32),
                        pltpu.VMEM((1, n_flat), F32)],
    )
    return pl.pallas_call(
        functools.partial(_fox_sample_body, n_grp=n_grp),
        grid_spec=grid_spec,
        out_shape=jax.ShapeDtypeStruct((db, nrow, HD), BF16),
        compiler_params=_cparams("parallel", "arbitrary"),
        name="fox_sample",
    )(page_table.reshape(-1), q, kn, vn, cum_col, cum_row, *([cache_k] * n_grp), *([cache_v] * n_grp),
      *([cache_sums] * n_grp))


def _hg_chunk(q, k, v, lf, s0, k_scr=None, b_scr=None):
    c = q.shape[0]
    if c > SUB:
        tri = (lax.broadcasted_iota(jnp.int32, (c, c), 1) <= lax.broadcasted_iota(jnp.int32, (c, c), 0)).astype(F32)
        b = jnp.dot(tri, lf, preferred_element_type=F32, precision=lax.Precision.HIGHEST)
    else:
        b = _cumsum_rows(lf)
    o = jnp.dot((q * jnp.exp(b)).astype(BF16), s0.astype(BF16), preferred_element_type=F32)
    vb = v.astype(BF16)

    row = lax.broadcasted_iota(jnp.int32, (c, 1), 0)
    a_mat = None
    half = SUB
    while 2 * half <= c:
        nblk = c // (2 * half)
        shift = (2 * half).bit_length() - 1
        split = jnp.broadcast_to(b.reshape(nblk, 2 * half, HD)[:, half - 1:half, :],
                                 (nblk, 2 * half, HD)).reshape(c, HD)
        right = (row & (2 * half - 1)) >= half
        qt = jnp.where(right, q * jnp.exp(b - split), 0.0).astype(BF16)
        kt = jnp.where(right, 0.0, k * jnp.exp(split - b)).astype(BF16)
        a_l = lax.dot_general(qt, kt, NT_DIMS, preferred_element_type=F32)
        ri = lax.broadcasted_iota(jnp.int32, (c, c), 0) >> shift
        ci = lax.broadcasted_iota(jnp.int32, (c, c), 1) >> shift
        a_l = jnp.where(ri == ci, a_l, 0.0)
        a_mat = a_l if a_mat is None else a_mat + a_l
        half *= 2

    if k_scr is not None:
        k_scr[...] = k
        b_scr[...] = b
        k_row = lambda r: k_scr[r:r + 1, :]
        b_row = lambda r: b_scr[r:r + 1, :]
    else:
        k_row = lambda r: k[r:r + 1, :]
        b_row = lambda r: b[r:r + 1, :]
    sub_row = lax.broadcasted_iota(jnp.int32, (SUB, 1), 0)
    if a_mat is None:
        ob = jnp.zeros((SUB, HD), F32)
        for s in range(SUB):
            a_s = jnp.sum(q * k_row(s) * jnp.exp(b - b_row(s)), axis=-1, keepdims=True)
            ob = ob + jnp.where(sub_row >= s, a_s, 0.0) * v[s:s + 1, :]
        o = o + ob
    else:
        lane = lax.broadcasted_iota(jnp.int32, (SUB, c), 1)
        blocks = []
        for blk in range(c // SUB):
            sl = slice(blk * SUB, (blk + 1) * SUB)
            qb_, bb_ = q[sl], b[sl]
            acc = jnp.zeros((SUB, c), F32)
            for s in range(SUB):
                r = blk * SUB + s
                a_s = jnp.sum(qb_ * k_row(r) * jnp.exp(bb_ - b_row(r)), axis=-1, keepdims=True)
                acc = jnp.where(lane == r, a_s, acc)
            blocks.append(jnp.where(lane <= sub_row + blk * SUB, acc, 0.0))
        a_mat = a_mat + jnp.concatenate(blocks, axis=0)
        o = o + jnp.dot(a_mat.astype(BF16), vb, preferred_element_type=F32)

    b_last = b[c - 1:c, :]
    kt = (k * jnp.exp(b_last - b)).astype(BF16)
    s_new = _row_to_col(jnp.exp(b_last)) * s0 + lax.dot_general(kt, vb, TN_DIMS, preferred_element_type=F32)
    return o, s_new


def _hg_gates(hq, hf, lb):
    sig = _sigmoid(hf)
    lf = jnp.log(lb + (1.0 - lb) * sig)
    k = (1.0 - lb) * _sigmoid(-hf)
    return _silu(hq), k, lf


def _hg_prompt_body(hq_ref, hf_ref, hi_ref, hg_ref, lb_ref, gn_ref, o_ref, s_ref, k_scr, b_scr, *, chunk):
    t = hq_ref.shape[0]
    n_heads = hq_ref.shape[1] // HD
    gn = gn_ref[...]

    def step(ci, states):
        rows = pl.ds(pl.multiple_of(ci * chunk, chunk), chunk)
        new_states = []
        for h in range(n_heads):
            cs = slice(h * HD, (h + 1) * HD)
            q, k, lf = _hg_gates(hq_ref[rows, cs], hf_ref[rows, cs], lb_ref[:, cs])
            o, s = _hg_chunk(q, k, hi_ref[rows, cs], lf, states[h], k_scr.at[h], b_scr.at[h])
            o_ref[rows, cs] = (_rms(o, gn) * _silu(hg_ref[rows, cs])).astype(o_ref.dtype)
            new_states.append(s)
        return tuple(new_states)

    states = lax.fori_loop(0, t // chunk, step, tuple(jnp.zeros((HD, HD), F32) for _ in range(n_heads)))
    for h in range(n_heads):
        s_ref[0, h] = states[h]


def _hg_prompt(z, lb, gn, nb, t, offs, heads_per_step=4):
    hw = heads_per_step * HD
    qb, fb, ib, gb = (offs[n] // hw for n in ("hq", "hf", "hi", "hg"))
    chunk = min(128, t)
    col = lambda base: pl.BlockSpec((t, hw), lambda b, h: (b, base + h))
    return pl.pallas_call(
        functools.partial(_hg_prompt_body, chunk=chunk),
        grid=(nb, HG_H // heads_per_step),
        in_specs=[col(qb), col(fb), col(ib), col(gb),
                  pl.BlockSpec((1, hw), lambda b, h: (0, h)),
                  pl.BlockSpec((1, HD), lambda b, h: (0, 0))],
        out_specs=[pl.BlockSpec((t, hw), lambda b, h: (b, h)),
                   pl.BlockSpec((1, heads_per_step, HD, HD), lambda b, h: (b, h, 0, 0))],
        out_shape=[jax.ShapeDtypeStruct((nb * t, HG_H * HD), BF16),
                   jax.ShapeDtypeStruct((nb, HG_H, HD, HD), F32)],
        scratch_shapes=[pltpu.VMEM((heads_per_step, chunk, HD), F32)] * 2,
        compiler_params=_cparams("parallel", "parallel"),
        name="hgrn2_prompt",
    )(z, z, z, z, lb, gn)


def _hg_sample_body(hq_ref, hf_ref, hi_ref, hg_ref, lb_ref, gn_ref, s0_ref, o_ref, s_ref, *, n_tok):
    valid = lax.broadcasted_iota(jnp.int32, (PAD_L, 1), 0) < n_tok
    gn = gn_ref[...]
    for i in range(hq_ref.shape[0]):
        for h in range(HG_H):
            cs = slice(h * HD, (h + 1) * HD)
            q, k, lf = _hg_gates(hq_ref[i, :, cs], hf_ref[i, :, cs], lb_ref[:, cs])
            k = jnp.where(valid, k, 0.0)
            lf = jnp.where(valid, lf, 0.0)
            o, s = _hg_chunk(q, k, hi_ref[i, :, cs], lf, s0_ref[0, i, h])
            s_ref[i, h] = s
            o_ref[i, :, cs] = (_rms(o, gn) * _silu(hg_ref[i, :, cs])).astype(o_ref.dtype)


def _hg_sample(zs, lb, gn, state, layer, offs, n_tok):
    db = zs.shape[0]
    w = HG_H * HD
    ns = 2 if db % 2 == 0 else 1
    col = lambda name: pl.BlockSpec((ns, PAD_L, w), lambda b: (b, 0, offs[name] // w))
    return pl.pallas_call(
        functools.partial(_hg_sample_body, n_tok=n_tok),
        grid=(db // ns,),
        in_specs=[col("hq"), col("hf"), col("hi"), col("hg"),
                  pl.BlockSpec((1, w), lambda b: (0, 0)),
                  pl.BlockSpec((1, HD), lambda b: (0, 0)),
                  pl.BlockSpec((1, ns, HG_H, HD, HD), lambda b: (layer, b, 0, 0, 0))],
        out_specs=[pl.BlockSpec((ns, PAD_L, w), lambda b: (b, 0, 0)),
                   pl.BlockSpec((ns, HG_H, HD, HD), lambda b: (b, 0, 0, 0))],
        out_shape=[jax.ShapeDtypeStruct((db, PAD_L, w), BF16),
                   jax.ShapeDtypeStruct((db, HG_H, HD, HD), F32)],
        compiler_params=_cparams("parallel"),
        name="hgrn2_sample",
    )(zs, zs, zs, zs, lb, gn, state)


def _lru_gates(xc, wa_ref, ba, wx_ref, bx, lam):
    nblk = xc.shape[1] // HD
    xb = xc.astype(BF16)
    pa = [jnp.dot(xb[:, i * HD:(i + 1) * HD], wa_ref[i], preferred_element_type=F32) for i in range(nblk)]
    px = [jnp.dot(xb[:, i * HD:(i + 1) * HD], wx_ref[i], preferred_element_type=F32) for i in range(nblk)]
    r = _sigmoid(jnp.concatenate(pa, axis=1) + ba)
    ig = _sigmoid(jnp.concatenate(px, axis=1) + bx)
    log_a = -LRU_C * r * _softplus(-lam)
    a = jnp.exp(log_a)
    u = jnp.sqrt(-jnp.tanh(log_a) * (1.0 + a * a)) * (ig * xc)
    return a, u


def _lru_prompt_body(lx_ref, ly_ref, cw_ref, cb_ref, wa_ref, ba_ref, wx_ref, bx_ref, lam_ref,
                     o_ref, h_ref, *, rt):
    t, wt = lx_ref.shape
    cw = cw_ref[...]
    sub_row = lax.broadcasted_iota(jnp.int32, (SUB, 1), 0)
    pos = lax.broadcasted_iota(jnp.int32, (rt, 1), 0) & (SUB - 1)

    def tile(ti, carry):
        h, tail = carry
        rows = pl.ds(pl.multiple_of(ti * rt, rt), rt)
        x = lx_ref[rows, :]
        xc = cb_ref[...] + x * cw[CONV_W - 1:CONV_W, :]
        for d in range(1, CONV_W):
            xr = pltpu.roll(x, d, 0)
            head = jnp.where(sub_row < d, pltpu.roll(tail, d, 0), xr[0:SUB])
            xs = jnp.concatenate([head, xr[SUB:]], axis=0) if rt > SUB else head
            xc = xc + xs * cw[CONV_W - 1 - d:CONV_W - d, :]
        a, u = _lru_gates(xc, wa_ref, ba_ref[...], wx_ref, bx_ref[...], lam_ref[...])
        d = 1
        while d < SUB:
            keep = pos >= d
            u = a * jnp.where(keep, pltpu.roll(u, d, 0), 0.0) + u
            a = a * jnp.where(keep, pltpu.roll(a, d, 0), 1.0)
            d *= 2
        gelu = _gelu_tanh(ly_ref[rows, :])
        for g in range(rt // SUB):
            sl = slice(g * SUB, (g + 1) * SUB)
            hs = a[sl] * h + u[sl]
            h = hs[SUB - 1:SUB, :]
            o_ref[pl.ds(pl.multiple_of(ti * rt, rt) + g * SUB, SUB), :] = (hs * gelu[sl]).astype(o_ref.dtype)
        return h, x[rt - SUB:rt, :]

    h, _ = lax.fori_loop(0, t // rt, tile, (jnp.zeros((1, wt), F32), jnp.zeros((SUB, wt), F32)))
    h_ref[0] = h


def _lru_prompt(z, p, nb, t, offs):
    w = p["lru_conv_b"].shape[1]
    wt = SEG
    nblk = wt // HD
    rt = min(128, t)
    xb, yb = offs["lx"] // wt, offs["ly"] // wt
    vec = pl.BlockSpec((1, wt), lambda b, c: (0, c))
    gate_w = pl.BlockSpec((nblk, HD, HD), lambda b, c: (c, 0, 0))
    return pl.pallas_call(
        functools.partial(_lru_prompt_body, rt=rt),
        grid=(nb, w // wt),
        in_specs=[pl.BlockSpec((t, wt), lambda b, c: (b, xb + c)),
                  pl.BlockSpec((t, wt), lambda b, c: (b, yb + c)),
                  pl.BlockSpec((CONV_W, wt), lambda b, c: (0, c)),
                  vec, gate_w, vec, gate_w, vec, vec],
        out_specs=[pl.BlockSpec((t, wt), lambda b, c: (b, c)),
                   pl.BlockSpec((1, 1, wt), lambda b, c: (b, 0, c))],
        out_shape=[jax.ShapeDtypeStruct((nb * t, w), BF16),
                   jax.ShapeDtypeStruct((nb, 1, w), F32)],
        compiler_params=_cparams("parallel", "parallel"),
        name="lru_prompt",
    )(z, z, p["lru_conv_w"], p["lru_conv_b"], p["lru_w_a"], p["lru_b_a"], p["lru_w_x"], p["lru_b_x"],
      p["lru_lambda"])


def _lru_sample_body(*refs, n_tok):
    lx = refs[0:n_tok]
    ly = refs[n_tok:2 * n_tok]
    buf = refs[2 * n_tok:2 * n_tok + CONV_W - 1]
    h0_ref, cw_ref, cb_ref, wa_ref, ba_ref, wx_ref, bx_ref, lam_ref = refs[2 * n_tok + CONV_W - 1:2 * n_tok + CONV_W + 7]
    outs = refs[2 * n_tok + CONV_W + 7:]
    o_refs, h_ref = outs[0:n_tok], outs[n_tok]
    cw = cw_ref[...]
    xp = [r[0] for r in buf] + [r[...] for r in lx]
    h = h0_ref[0]
    for t in range(n_tok):
        xc = cb_ref[...]
        for j in range(CONV_W):
            xc = xc + xp[t + j] * cw[j:j + 1, :]
        a, u = _lru_gates(xc, wa_ref, ba_ref[...], wx_ref, bx_ref[...], lam_ref[...])
        h = a * h + u
        o_refs[t][...] = (h * _gelu_tanh(ly[t][...])).astype(o_refs[t].dtype)
    h_ref[...] = h


def _lru_sample(zs2, p, state_lru, state_conv, layer, offs, n_tok, nz):
    db = zs2.shape[0]
    w = p["lru_conv_b"].shape[1]
    wt = SEG
    nblk = wt // HD
    sc = state_conv.reshape(state_conv.shape[0], db, (CONV_W - 1) * w)
    tok = lambda name, l: pl.BlockSpec((db, wt), lambda c: (0, (l * nz + offs[name]) // wt + c))
    vec = pl.BlockSpec((1, wt), lambda c: (0, c))
    gate_w = pl.BlockSpec((nblk, HD, HD), lambda c: (c, 0, 0))
    in_specs = ([tok("lx", l) for l in range(n_tok)] + [tok("ly", l) for l in range(n_tok)]
                + [pl.BlockSpec((1, db, wt), lambda c, j=j: (layer, 0, j * (w // wt) + c)) for j in range(CONV_W - 1)]
                + [pl.BlockSpec((1, db, wt), lambda c: (layer, 0, c)),
                   pl.BlockSpec((CONV_W, wt), lambda c: (0, c)),
                   vec, gate_w, vec, gate_w, vec, vec])
    out_spec = pl.BlockSpec((db, wt), lambda c: (0, c))
    outs = pl.pallas_call(
        functools.partial(_lru_sample_body, n_tok=n_tok),
        grid=(w // wt,),
        in_specs=in_specs,
        out_specs=[out_spec] * (n_tok + 1),
        out_shape=[jax.ShapeDtypeStruct((db, w), BF16)] * n_tok + [jax.ShapeDtypeStruct((db, w), F32)],
        compiler_params=_cparams("parallel"),
        name="lru_sample",
    )(*([zs2] * (2 * n_tok) + [sc] * (CONV_W - 1)
        + [state_lru, p["lru_conv_w"], p["lru_conv_b"], p["lru_w_a"], p["lru_b_a"], p["lru_w_x"], p["lru_b_x"],
           p["lru_lambda"]]))
    return jnp.stack(outs[:n_tok], axis=1).reshape(db * n_tok, w), outs[n_tok]


def _xattn_prompt_body(q_ref, k_ref, v_ref, o_ref):
    scale = HD ** -0.5
    for h in range(X_H):
        cs = slice(h * HD, (h + 1) * HD)
        q = (q_ref[:, cs] * scale).astype(BF16)
        s = lax.dot_general(q, k_ref[0, :, cs].astype(BF16), NT_DIMS, preferred_element_type=F32)
        p = jnp.exp(s - jnp.max(s, axis=-1, keepdims=True))
        l = jnp.sum(p, axis=-1, keepdims=True)
        o = jnp.dot(p.astype(BF16), v_ref[0, :, cs].astype(BF16), preferred_element_type=F32) / l
        o_ref[:, cs] = o.astype(o_ref.dtype)


def _xattn_prompt(q, mk, mv, nb, t):
    w = X_H * HD
    m = mk.shape[1]
    tq = min(512, t)
    return pl.pallas_call(
        _xattn_prompt_body,
        grid=(nb, t // tq),
        in_specs=[pl.BlockSpec((tq, w), lambda b, i: (b * (t // tq) + i, 0)),
                  pl.BlockSpec((1, m, w), lambda b, i: (b, 0, 0)),
                  pl.BlockSpec((1, m, w), lambda b, i: (b, 0, 0))],
        out_specs=pl.BlockSpec((tq, w), lambda b, i: (b * (t // tq) + i, 0)),
        out_shape=jax.ShapeDtypeStruct((nb * t, w), BF16),
        compiler_params=_cparams("parallel", "arbitrary"),
        name="xattn_prompt",
    )(q, mk, mv)


def _xattn_sample_body(q_ref, k_ref, v_ref, o_ref, *, n_seq):
    scale = HD ** -0.5
    for i in range(n_seq):
        q = (q_ref[i] * scale).astype(BF16)
        s = lax.dot_general(q, k_ref[0, i].astype(BF16), NT_DIMS, preferred_element_type=F32)
        r = lax.broadcasted_iota(jnp.int32, s.shape, 0)
        c = lax.broadcasted_iota(jnp.int32, s.shape, 1)
        s = jnp.where((r & (X_H - 1)) == (c & (X_H - 1)), s, NEG_INF)
        p = jnp.exp(s - jnp.max(s, axis=-1, keepdims=True))
        l = jnp.sum(p, axis=-1, keepdims=True)
        o = jnp.dot(p.astype(BF16), v_ref[0, i].astype(BF16), preferred_element_type=F32) / l
        o_ref[i] = o.astype(o_ref.dtype)


def _xattn_sample(q, cache_mk, cache_mv, layer):
    db, nrow, _ = q.shape
    mx = cache_mk.shape[2]
    n_seq = min(8, db)
    kv = pl.BlockSpec((1, n_seq, mx, HD), lambda b: (layer, b, 0, 0))
    return pl.pallas_call(
        functools.partial(_xattn_sample_body, n_seq=n_seq),
        grid=(db // n_seq,),
        in_specs=[pl.BlockSpec((n_seq, nrow, HD), lambda b: (b, 0, 0)), kv, kv],
        out_specs=pl.BlockSpec((n_seq, nrow, HD), lambda b: (b, 0, 0)),
        out_shape=jax.ShapeDtypeStruct((db, nrow, HD), BF16),
        compiler_params=_cparams("parallel"),
        name="xattn_sample",
    )(q, cache_mk, cache_mv)


def _segment_offsets(d):
    fox_w, hg_w, lru_w = FOX_H * HD, HG_H * HD, HG_H * HD
    names = [("fq", fox_w), ("fk", fox_w), ("fv", fox_w), ("hq", hg_w), ("hf", hg_w), ("hi", hg_w), ("hg", hg_w),
             ("lx", lru_w), ("ly", lru_w), ("ga", d), ("gb", d), ("gc", d), ("ff", FF_SEG)]
    offs, o = {}, 0
    for n, wdt in names:
        offs[n] = o
        o += wdt
    return offs, o


def _rearrange_body(main_ref, extra_ref, o_ref, *, n_plain, n_shift):
    c = pl.program_id(2)
    tr, tw = o_ref.shape

    @pl.when(c < n_plain)
    def _():
        o_ref[...] = main_ref[...].T.astype(o_ref.dtype)

    @pl.when((c >= n_plain) & (c < n_shift))
    def _():
        y = jnp.concatenate([main_ref[...], extra_ref[...]], axis=0)
        o_ref[...] = y[FOX_H:FOX_H + tw].T.astype(o_ref.dtype)

    @pl.when(c == n_shift)
    def _():
        y = jnp.concatenate([extra_ref[...], jnp.zeros((tw - FOX_H, tr), F32)], axis=0)
        o_ref[...] = y.T.astype(o_ref.dtype)

    @pl.when(c > n_shift)
    def _():
        o_ref[...] = jnp.zeros_like(o_ref)


def _rearrange_w_in(w_in, nz):
    depth, d, n_in = w_in.shape
    assert FOX_H == SUB
    fw3 = 3 * FOX_H * HD
    off_ff = n_in - FOX_H
    tw = next(t for t in (1024, 512, 256, 128) if fw3 % t == 0 and off_ff % t == 0 and nz % t == 0)
    tr = _row_tile(d, cap=512)
    n_plain, n_shift = fw3 // tw, off_ff // tw

    def main_map(l, r, c):
        return (l, jnp.minimum(c, n_shift - 1), r)

    def extra_map(l, r, c):
        shifted = (c >= n_plain) & (c < n_shift)
        return (l, jnp.where(shifted, (c + 1) * (tw // SUB), fw3 // SUB), r)

    w_t = jnp.swapaxes(w_in, 1, 2)
    return pl.pallas_call(
        functools.partial(_rearrange_body, n_plain=n_plain, n_shift=n_shift),
        grid=(depth, d // tr, nz // tw),
        in_specs=[pl.BlockSpec((None, tw, tr), main_map), pl.BlockSpec((None, SUB, tr), extra_map)],
        out_specs=pl.BlockSpec((None, tr, tw), lambda l, r, c: (l, r, c)),
        out_shape=jax.ShapeDtypeStruct((depth, d, nz), BF16),
        compiler_params=_cparams("parallel", "parallel", "arbitrary"),
        name="rearrange_w_in",
    )(w_t, w_t)


def _pad_tokens(x, db, n_tok):
    return jnp.pad(x.reshape(db, n_tok, x.shape[1]), ((0, 0), (0, PAD_L - n_tok), (0, 0)))


def kernel(x_prompt, x_sample, mem_prompt, cache_k, cache_v, cache_logf, state_hgrn, state_lru, state_conv, cache_mem_k, cache_mem_v, page_table, ffn1_norm, ffn1_w_gate, ffn1_w_up, ffn1_w_down, mix_norm, w_in, fox_b_f, fox_q_norm, fox_k_norm, hg_lb_param, hg_out_norm, lru_conv_w, lru_conv_b, lru_w_a, lru_b_a, lru_w_x, lru_b_x, lru_lambda, w_br_fox, w_br_hg, w_br_lru, w_out, xa_norm, xa_mem_norm, xa_w_q, xa_w_kv, xa_q_norm, xa_k_norm, xa_w_o, ffn2_norm, ffn2_w_gate, ffn2_w_up, ffn2_w_down):
    nb, t, d = x_prompt.shape
    db, n_tok, _ = x_sample.shape
    depth = w_in.shape[0]
    n_mem = mem_prompt.shape[1]
    np_rows = nb * t
    fw, xw = FOX_H * HD, X_H * HD
    lw = lru_conv_b.shape[1]
    assert n_tok <= PAD_L and n_tok >= CONV_W - 1 and t >= CONV_W - 1
    offs, nz = _segment_offsets(d)

    sm = jax.nn.softmax(hg_lb_param.astype(F32), axis=0)
    hg_lb = jnp.cumsum(sm, axis=0) - sm[0]

    x = jnp.concatenate([x_prompt.reshape(np_rows, d), x_sample.reshape(db * n_tok, d)], axis=0)
    mem = mem_prompt.reshape(nb * n_mem, d)
    cmk = cache_mem_k.reshape(depth, db, n_mem * X_H, HD)
    cmv = cache_mem_v.reshape(depth, db, n_mem * X_H, HD)

    cache_sums = _fox_cache_sums(cache_logf)
    w_in_r = _rearrange_w_in(w_in, nz)
    w16 = {name: w.astype(BF16) for name, w in dict(
        ffn1_g=ffn1_w_gate, ffn1_u=ffn1_w_up, ffn1_d=ffn1_w_down, ffn2_g=ffn2_w_gate, ffn2_u=ffn2_w_up,
        ffn2_d=ffn2_w_down, br_fox=w_br_fox, br_hg=w_br_hg, br_lru=w_br_lru, out=w_out,
        xa_q=xa_w_q, xa_kv=xa_w_kv, xa_o=xa_w_o).items()}

    outs = {n: [] for n in ("pk", "pv", "plf", "ps", "ph", "pc", "pmk", "pmv", "sk", "sv", "slf", "ss", "sh", "sc")}
    for l in range(depth):
        bf = lambda a: a[l].astype(BF16)
        row = lambda a: a[l].reshape(1, -1).astype(F32)
        lru_p = dict(lru_conv_w=lru_conv_w[l], lru_conv_b=row(lru_conv_b), lru_w_a=bf(lru_w_a), lru_b_a=row(lru_b_a),
                     lru_w_x=bf(lru_w_x), lru_b_x=row(lru_b_x), lru_lambda=row(lru_lambda))
        fox_bias = jnp.pad(row(fox_b_f), ((0, 0), (0, LANES - FOX_H)))
        head_gain = jnp.concatenate([jnp.tile(row(fox_q_norm), (1, FOX_H)), jnp.tile(row(fox_k_norm), (1, FOX_H)),
                                     jnp.ones((1, nz - 2 * fw), F32)], axis=1)

        x = _ffn(x, ffn1_norm[l], w16["ffn1_g"], w16["ffn1_u"], w16["ffn1_d"], l)
        z = _norm_matmul(x, mix_norm[l], w_in_r, l, head_gain, n_norm_cols=2 * fw)
        zs = _pad_tokens(z[np_rows:], db, n_tok)

        plf, c_row = _fox_prep(z, fox_bias, nb, t, offs["ff"])
        of_p = _fox_prompt(z, c_row, nb, t, offs)
        slf, cum = _fox_gates(zs, fox_bias, offs["ff"], n_tok)
        slf = slf.reshape(db, PAD_L, LANES)
        cum = cum.reshape(db, PAD_L, LANES)[:, :, :FOX_H]
        heads = lambda name, rows: zs[:, :rows, offs[name]:offs[name] + fw].reshape(db, rows * FOX_H, HD)
        of_s = _fox_sample(heads("fq", n_tok), heads("fk", PAD_L), heads("fv", PAD_L),
                           cum[:, :n_tok].reshape(db, n_tok * FOX_H, 1), cum.reshape(db, 1, PAD_L * FOX_H),
                           cache_k, cache_v, cache_sums, page_table, l).reshape(db * n_tok, fw)
        oh_p, ps = _hg_prompt(z, row(hg_lb), row(hg_out_norm), nb, t, offs)
        oh_s, ss = _hg_sample(zs, row(hg_lb), row(hg_out_norm), state_hgrn, l, offs, n_tok)
        ol_p, ph = _lru_prompt(z, lru_p, nb, t, offs)
        ol_s, sh = _lru_sample(zs.reshape(db, PAD_L * nz), lru_p, state_lru, state_conv, l, offs, n_tok, nz)

        unpad = lambda a: a[:, :n_tok].reshape(db * n_tok, a.shape[2])
        o_fox = jnp.concatenate([of_p, of_s], axis=0)
        o_hg = jnp.concatenate([oh_p, unpad(oh_s)], axis=0)
        o_lru = jnp.concatenate([ol_p, ol_s], axis=0)
        y = _merge(o_fox, o_hg, o_lru, w16["br_fox"], w16["br_hg"], w16["br_lru"], z, offs["ga"], l)
        x = _matmul_residual(y, w16["out"], x, l)

        xq = _norm_matmul(x, xa_norm[l], w16["xa_q"], l, jnp.tile(row(xa_q_norm), (1, X_H)), n_norm_cols=xw)
        kv_gain = jnp.concatenate([jnp.tile(row(xa_k_norm), (1, X_H)), jnp.ones((1, xw), F32)], axis=1)
        mkv = _norm_matmul(mem, xa_mem_norm[l], w16["xa_kv"], l, kv_gain, n_norm_cols=xw)
        mk, mv = mkv[:, :xw].reshape(nb, n_mem, xw), mkv[:, xw:].reshape(nb, n_mem, xw)
        ox_p = _xattn_prompt(xq[:np_rows], mk, mv, nb, t)
        ox_s = _xattn_sample(xq[np_rows:].reshape(db, n_tok * X_H, HD), cmk, cmv, l).reshape(db * n_tok, xw)
        x = _matmul_residual(jnp.concatenate([ox_p, ox_s], axis=0), w16["xa_o"], x, l)

        x = _ffn(x, ffn2_norm[l], w16["ffn2_g"], w16["ffn2_u"], w16["ffn2_d"], l)

        zp, zsm = z[:np_rows], z[np_rows:]
        outs["pk"].append(zp[:, offs["fk"]:offs["fk"] + fw])
        outs["pv"].append(zp[:, offs["fv"]:offs["fv"] + fw])
        outs["plf"].append(plf)
        outs["ps"].append(ps)
        outs["ph"].append(ph.reshape(nb, lw))
        outs["pc"].append(zp[:, offs["lx"]:offs["lx"] + lw].reshape(nb, t, lw)[:, t - (CONV_W - 1):])
        outs["pmk"].append(mk.reshape(nb, n_mem, X_H, HD))
        outs["pmv"].append(mv.reshape(nb, n_mem, X_H, HD))
        outs["sk"].append(zsm[:, offs["fk"]:offs["fk"] + fw].reshape(db, n_tok, FOX_H, HD))
        outs["sv"].append(zsm[:, offs["fv"]:offs["fv"] + fw].reshape(db, n_tok, FOX_H, HD))
        outs["slf"].append(slf[:, :n_tok, :FOX_H])
        outs["ss"].append(ss)
        outs["sh"].append(sh)
        outs["sc"].append(zsm[:, offs["lx"]:offs["lx"] + lw].reshape(db, n_tok, lw)[:, n_tok - (CONV_W - 1):])

    st = lambda n: jnp.stack(outs[n])
    return (x[:np_rows].reshape(nb, t, d), x[np_rows:].reshape(db, n_tok, d),
            st("pk").reshape(depth, nb, t, FOX_H, HD), st("pv").reshape(depth, nb, t, FOX_H, HD),
            st("plf"), st("ps"), st("ph"), st("pc"), st("pmk"), st("pmv"),
            st("sk"), st("sv"), st("slf"), st("ss"), st("sh"), st("sc"))
```
